```python
import math
import jax
import jax.numpy as jnp
from jax import lax
import numpy as np

D_MODEL = 4096
BATCH = 2
SEQ = 4096
DEPTH = 2

CTX_LEN = 256
GRID_W = 64
ROPE_BASE = 10000.0
LN_EPS = 1e-5
DEEPNORM_ALPHA = (2 * DEPTH) ** 0.25
DEEPNORM_BETA = (8 * DEPTH) ** -0.25

D_RNN = D_MODEL // 2
RNN_BLOCKS = 16
RNN_BLOCK = D_RNN // RNN_BLOCKS
CONV_W = 4
CONV_LEFT = (CONV_W - 1) // 2
LRU_C = 8.0

WA_HEAD_DIM = 128
WA_HEADS = D_MODEL // 256
WA_KV_HEADS = WA_HEADS // 4
WINDOW = 128
BLOCK = 128

DA_HEAD_DIM = 64
DA_HEADS = D_MODEL // 256
DA_V_DIM = 2 * DA_HEAD_DIM
Q_BLOCK = 128

N_EXPERTS = 32
TOP_K = 4
D_EXPERT = D_MODEL // 8
SWIGLU_LIMIT = 7.0
SWIGLU_ALPHA = 1.702

N_BRANCHES = 3
IN_SIZES = (D_RNN, WA_HEADS * WA_HEAD_DIM, WA_KV_HEADS * WA_HEAD_DIM, WA_KV_HEADS * WA_HEAD_DIM,
            DA_HEADS * 2 * DA_HEAD_DIM, DA_HEADS * 2 * DA_HEAD_DIM, DA_HEADS * DA_V_DIM,
            N_BRANCHES * D_MODEL)
D_IN = sum(IN_SIZES)

kernel_name = 'hybrid_rglru_swa_diffattn_moe_dit'


def layer_norm(x):
    xf = x.astype(jnp.float32)
    mu = jnp.mean(xf, -1, keepdims=True)
    var = jnp.mean(jnp.square(xf - mu), -1, keepdims=True)
    return ((xf - mu) * lax.rsqrt(var + LN_EPS)).astype(x.dtype)


def layer_norm_affine(x, g, b):
    return (layer_norm(x) * g + b).astype(x.dtype)


def modulate(h, shift, scale):
    return h * (1 + scale) + shift


def lambda_init(layer):
    return 0.8 - 0.6 * math.exp(-0.3 * layer)


def split_columns(t, sizes):
    outs, start = [], 0
    for n in sizes:
        outs.append(t[..., start:start + n])
        start += n
    return outs


def axial_rope(pos_row, pos_col, head_dim):
    n = head_dim // 4
    inv = ROPE_BASE ** (-jnp.arange(n, dtype=jnp.float32) / n)
    ang = jnp.concatenate([pos_row[:, None].astype(jnp.float32) * inv,
                           pos_col[:, None].astype(jnp.float32) * inv], -1)
    return jnp.cos(ang), jnp.sin(ang)


def apply_rope(t, cos, sin):
    t1, t2 = jnp.split(t, 2, axis=-1)
    cs = cos[None, :, None, :].astype(t.dtype)
    sn = sin[None, :, None, :].astype(t.dtype)
    return jnp.concatenate([t1 * cs - t2 * sn, t1 * sn + t2 * cs], -1)


def depthwise_conv(u, w, b):
    L = u.shape[1]
    up = jnp.pad(u, ((0, 0), (CONV_LEFT, CONV_W - 1 - CONV_LEFT), (0, 0)))
    return b + sum(w[k] * up[:, k:k + L] for k in range(CONV_W))


def rglru_coeffs(v, w_r, b_r, w_i, b_i, lam):
    B, L, _ = v.shape
    vb = v.reshape(B, L, RNN_BLOCKS, RNN_BLOCK)
    r = jax.nn.sigmoid(jnp.einsum('blnd,nde->blne', vb, w_r.astype(jnp.float32)).reshape(B, L, D_RNN) + b_r)
    i = jax.nn.sigmoid(jnp.einsum('blnd,nde->blne', vb, w_i.astype(jnp.float32)).reshape(B, L, D_RNN) + b_i)
    log_a = -LRU_C * r * jax.nn.softplus(-lam.astype(jnp.float32))
    return jnp.exp(log_a), jnp.sqrt(-jnp.expm1(2.0 * log_a)) * (i * v)


def _scan_op(e1, e2):
    a1, b1 = e1
    a2, b2 = e2
    return a1 * a2, a2 * b1 + b2


def linear_scan(a, b, h0, reverse):
    t = -1 if reverse else 0
    b = b.at[:, t].add(a[:, t] * h0)
    return lax.associative_scan(_scan_op, (a, b), reverse=reverse, axis=1)[1]


def rglru_branch(ux, uc, p, with_ctx):
    vx = depthwise_conv(ux, p['conv_w'], p['conv_b']).astype(jnp.float32)
    vc = depthwise_conv(uc, p['conv_w'], p['conv_b']).astype(jnp.float32)
    h0 = jnp.zeros((ux.shape[0], D_RNN), jnp.float32)
    hx = 0.0
    ctx_scans = []
    for d, reverse in enumerate((False, True)):
        gates = (p['lru_wr'][d], p['lru_br'][d], p['lru_wi'][d], p['lru_bi'][d], p['lru_lambda'][d])
        h_ctx = linear_scan(*rglru_coeffs(vc, *gates), h0, reverse)
        h_final = h_ctx[:, 0] if reverse else h_ctx[:, -1]
        hx = hx + linear_scan(*rglru_coeffs(vx, *gates), h_final, reverse)
        ctx_scans.append(h_ctx)
    yc = (ctx_scans[0] + ctx_scans[1]).astype(uc.dtype) if with_ctx else None
    return hx.astype(ux.dtype), yc


def band_blocks(t):
    B, S, H, d = t.shape
    tb = t.reshape(B, S // BLOCK, BLOCK, H, d)
    tp = jnp.pad(tb, ((0, 0), (1, 1), (0, 0), (0, 0), (0, 0)))
    return jnp.concatenate([tp[:, :-2], tp[:, 1:-1], tp[:, 2:]], axis=2)


def window_gqa(q, k, v, qc, kc, vc, sink, with_ctx):
    B, S, _, dh = q.shape
    C = kc.shape[1]
    nb = S // BLOCK
    G = WA_HEADS // WA_KV_HEADS
    nk = 3 * BLOCK
    scale = dh ** -0.5
    sink_f = sink.astype(jnp.float32).reshape(WA_KV_HEADS, G)
    qb = q.reshape(B, nb, BLOCK, WA_KV_HEADS, G, dh)
    kb, vb = band_blocks(k), band_blocks(v)
    s_loc = jnp.einsum('bnqhgd,bnkhd->bnhgqk', qb, kb).astype(jnp.float32) * scale
    s_ctx = jnp.einsum('bnqhgd,bkhd->bnhgqk', qb, kc).astype(jnp.float32) * scale
    qpos = jnp.arange(S).reshape(nb, BLOCK)
    kpos = (jnp.arange(nb)[:, None] - 1) * BLOCK + jnp.arange(nk)[None, :]
    kp = kpos[:, None, :]
    valid = (kp >= 0) & (kp < S) & (jnp.abs(kp - qpos[:, :, None]) <= WINDOW)
    s_loc = jnp.where(valid[None, :, None, None], s_loc, -jnp.inf)
    s_sink = jnp.broadcast_to(sink_f[None, None, :, :, None, None], s_loc.shape[:-1] + (1,))
    pr = jax.nn.softmax(jnp.concatenate([s_loc, s_ctx, s_sink], -1), axis=-1).astype(v.dtype)
    out = (jnp.einsum('bnhgqk,bnkhd->bnqhgd', pr[..., :nk], vb)
           + jnp.einsum('bnhgqk,bkhd->bnqhgd', pr[..., nk:nk + C], vc))
    yx = out.reshape(B, S, WA_HEADS * dh)
    yc = None
    if with_ctx:
        qcg = qc.reshape(B, C, WA_KV_HEADS, G, dh)
        s = jnp.einsum('bqhgd,bkhd->bhgqk', qcg, kc).astype(jnp.float32) * scale
        s_sink_c = jnp.broadcast_to(sink_f[None, :, :, None, None], s.shape[:-1] + (1,))
        pc = jax.nn.softmax(jnp.concatenate([s, s_sink_c], -1), axis=-1).astype(vc.dtype)
        yc = jnp.einsum('bhgqk,bkhd->bqhgd', pc[..., :C], vc).reshape(B, C, WA_HEADS * dh)
    return yx, yc


def diff_attention(q, k, v, qc, kc, vc, p, lam_init, with_ctx):
    B, S, H, _, dh = q.shape
    scale = dh ** -0.5
    lam = (jnp.exp(jnp.sum(p['da_lq1'].astype(jnp.float32) * p['da_lk1']))
           - jnp.exp(jnp.sum(p['da_lq2'].astype(jnp.float32) * p['da_lk2'])) + lam_init)

    def attend(qb, kk, vv):
        s = jnp.einsum('bqhmd,bkhmd->bmhqk', qb, kk).astype(jnp.float32) * scale
        a = jax.nn.softmax(s, axis=-1)
        w = (a[:, 0] - lam * a[:, 1]).astype(vv.dtype)
        return jnp.einsum('bhqk,bkhd->bqhd', w, vv)

    def sub_norm(o):
        of = o.astype(jnp.float32)
        of = of * lax.rsqrt(jnp.mean(jnp.square(of), -1, keepdims=True) + LN_EPS) * p['da_norm_g'] * (1.0 - lam_init)
        return of.reshape(o.shape[0], o.shape[1], H * DA_V_DIM).astype(o.dtype)

    k_all = jnp.concatenate([k, kc], axis=1)
    v_all = jnp.concatenate([v, vc], axis=1)
    nb = S // Q_BLOCK
    q_blocks = q.reshape(B, nb, Q_BLOCK, H, 2, dh).swapaxes(0, 1)
    ox = lax.map(lambda qb: attend(qb, k_all, v_all), q_blocks)
    ox = ox.swapaxes(0, 1).reshape(B, S, H, DA_V_DIM)
    yc = sub_norm(attend(qc, kc, vc)) if with_ctx else None
    return sub_norm(ox), yc


def merge_branches(ya, yb, yc, g, p):
    ga, gb, gc = jnp.split(jax.nn.sigmoid(g), N_BRANCHES, axis=-1)
    m = ga * (ya @ p['w_br_a']) + gb * (yb @ p['w_br_b']) + gc * (yc @ p['w_br_c'])
    return m @ p['w_out'] + p['b_out']


def token_mixing(hx, hc, p, rope_wa, rope_da, lam_init, with_ctx):
    B, S, _ = hx.shape
    C = hc.shape[1]
    ua_x, qa_x, ka_x, va_x, qd_x, kd_x, vd_x, g_x = split_columns(hx @ p['w_in'], IN_SIZES)
    ua_c, qa_c, ka_c, va_c, qd_c, kd_c, vd_c, g_c = split_columns(hc @ p['w_in'], IN_SIZES)
    ya_x, ya_c = rglru_branch(ua_x, ua_c, p, with_ctx)
    yb_x, yb_c = window_gqa(
        apply_rope(qa_x.reshape(B, S, WA_HEADS, WA_HEAD_DIM), *rope_wa),
        apply_rope(ka_x.reshape(B, S, WA_KV_HEADS, WA_HEAD_DIM), *rope_wa),
        va_x.reshape(B, S, WA_KV_HEADS, WA_HEAD_DIM),
        qa_c.reshape(B, C, WA_HEADS, WA_HEAD_DIM),
        ka_c.reshape(B, C, WA_KV_HEADS, WA_HEAD_DIM),
        va_c.reshape(B, C, WA_KV_HEADS, WA_HEAD_DIM),
        p['wa_sink'], with_ctx)
    qd = apply_rope(qd_x.reshape(B, S, 2 * DA_HEADS, DA_HEAD_DIM), *rope_da).reshape(B, S, DA_HEADS, 2, DA_HEAD_DIM)
    kd = apply_rope(kd_x.reshape(B, S, 2 * DA_HEADS, DA_HEAD_DIM), *rope_da).reshape(B, S, DA_HEADS, 2, DA_HEAD_DIM)
    yc_x, yc_c = diff_attention(
        qd, kd, vd_x.reshape(B, S, DA_HEADS, DA_V_DIM),
        qd_c.reshape(B, C, DA_HEADS, 2, DA_HEAD_DIM),
        kd_c.reshape(B, C, DA_HEADS, 2, DA_HEAD_DIM),
        vd_c.reshape(B, C, DA_HEADS, DA_V_DIM),
        p, lam_init, with_ctx)
    mx = merge_branches(ya_x, yb_x, yc_x, g_x, p)
    mc = merge_branches(ya_c, yb_c, yc_c, g_c, p) if with_ctx else None
    return mx, mc


def moe_ffn(h, q):
    logits = (h @ q['router_w'] + q['router_b']).astype(jnp.float32)
    top_v, top_i = lax.top_k(logits, TOP_K)
    w = jax.nn.softmax(top_v, axis=-1)
    combine = jnp.einsum('tk,tke->te', w, jax.nn.one_hot(top_i, N_EXPERTS, dtype=jnp.float32)).astype(h.dtype)
    out = jnp.zeros_like(h)
    for e in range(N_EXPERTS):
        gate = jnp.minimum(h @ q['w_gate'][e] + q['b_gate'][e], SWIGLU_LIMIT)
        up = jnp.clip(h @ q['w_up'][e] + q['b_up'][e], -SWIGLU_LIMIT, SWIGLU_LIMIT)
        y = ((up + 1) * gate * jax.nn.sigmoid(SWIGLU_ALPHA * gate)) @ q['w_down'][e] + q['b_down'][e]
        out = out + combine[:, e:e + 1] * y
    return out


def setup_inputs(seed: int = 0) -> dict:
    key = jax.random.key(seed)
    keys = jax.random.split(key, 40)
    f32 = jnp.float32
    L, D, E, F = DEPTH, D_MODEL, N_EXPERTS, D_EXPERT
    wa_w = WA_HEADS * WA_HEAD_DIM
    da_w = DA_HEADS * DA_V_DIM

    def nrm(i, shape, scale):
        return jax.random.normal(keys[i], shape, f32) * scale

    a0 = jax.random.uniform(keys[13], (L, 2, D_RNN), f32, 0.9, 0.999) ** (1.0 / LRU_C)
    return {
        'x': nrm(0, (BATCH, SEQ, D), 1.0),
        'c': nrm(1, (BATCH, D), 1.0),
        'ctx': nrm(2, (BATCH, CTX_LEN, D), 1.0),
        'c_ctx': nrm(3, (D,), 1.0),
        'w_ada': nrm(4, (L, D, 6 * D), 0.5 * D ** -0.5),
        'b_ada': nrm(5, (L, 6 * D), 0.01),
        'w_in': nrm(6, (L, D, D_IN), D ** -0.5),
        'conv_w': nrm(7, (L, CONV_W, D_RNN), CONV_W ** -0.5),
        'conv_b': nrm(8, (L, D_RNN), 0.01),
        'lru_wr': nrm(9, (L, 2, RNN_BLOCKS, RNN_BLOCK, RNN_BLOCK), RNN_BLOCK ** -0.5),
        'lru_br': nrm(10, (L, 2, D_RNN), 0.01),
        'lru_wi': nrm(11, (L, 2, RNN_BLOCKS, RNN_BLOCK, RNN_BLOCK), RNN_BLOCK ** -0.5),
        'lru_bi': nrm(12, (L, 2, D_RNN), 0.01),
        'lru_lambda': jnp.log(a0) - jnp.log1p(-a0),
        'wa_sink': nrm(14, (L, WA_HEADS), 0.5),
        'da_lq1': nrm(15, (L, DA_HEAD_DIM), 0.1),
        'da_lk1': nrm(16, (L, DA_HEAD_DIM), 0.1),
        'da_lq2': nrm(17, (L, DA_HEAD_DIM), 0.1),
        'da_lk2': nrm(18, (L, DA_HEAD_DIM), 0.1),
        'da_norm_g': 1.0 + nrm(19, (L, DA_V_DIM), 0.01),
        'w_br_a': nrm(20, (L, D_RNN, D), DEEPNORM_BETA * D_RNN ** -0.5),
        'w_br_b': nrm(21, (L, wa_w, D), DEEPNORM_BETA * wa_w ** -0.5),
        'w_br_c': nrm(22, (L, da_w, D), DEEPNORM_BETA * da_w ** -0.5),
        'w_out': nrm(23, (L, D, D), DEEPNORM_BETA * D ** -0.5),
        'b_out': nrm(24, (L, D), 0.01),
        'ln1_g': 1.0 + nrm(25, (L, D), 0.01),
        'ln1_b': nrm(26, (L, D), 0.01),
        'router_w': nrm(27, (L, D, E), D ** -0.5),
        'router_b': nrm(28, (L, E), 0.01),
        'exp_w_gate': nrm(29, (L, E, D, F), D ** -0.5),
        'exp_b_gate': nrm(30, (L, E, F), 0.01),
        'exp_w_up': nrm(31, (L, E, D, F), D ** -0.5),
        'exp_b_up': nrm(32, (L, E, F), 0.01),
        'exp_w_down': nrm(33, (L, E, F, D), DEEPNORM_BETA * F ** -0.5),
        'exp_b_down': nrm(34, (L, E, D), 0.01),
        'ln2_g': 1.0 + nrm(35, (L, D), 0.01),
        'ln2_b': nrm(36, (L, D), 0.01),
    }


def reference(x, c, ctx, c_ctx, w_ada, b_ada, w_in, conv_w, conv_b, lru_wr, lru_br, lru_wi, lru_bi,
              lru_lambda, wa_sink, da_lq1, da_lk1, da_lq2, da_lk2, da_norm_g, w_br_a, w_br_b, w_br_c,
              w_out, b_out, ln1_g, ln1_b, router_w, router_b, exp_w_gate, exp_b_gate, exp_w_up, exp_b_up,
              exp_w_down, exp_b_down, ln2_g, ln2_b):
    B, S, D = x.shape
    C = ctx.shape[1]
    rows = S // GRID_W
    pos_row = jnp.repeat(jnp.arange(rows, dtype=jnp.int32), GRID_W)
    pos_col = jnp.tile(jnp.arange(GRID_W, dtype=jnp.int32), rows)
    rope_wa = axial_rope(pos_row, pos_col, WA_HEAD_DIM)
    rope_da = axial_rope(pos_row, pos_col, DA_HEAD_DIM)
    silu_c = jax.nn.silu(c)
    silu_cc = jax.nn.silu(c_ctx)
    for l in range(DEPTH):
        with_ctx = l < DEPTH - 1
        p = {'w_in': w_in[l], 'conv_w': conv_w[l], 'conv_b': conv_b[l],
             'lru_wr': lru_wr[l], 'lru_br': lru_br[l], 'lru_wi': lru_wi[l], 'lru_bi': lru_bi[l],
             'lru_lambda': lru_lambda[l], 'wa_sink': wa_sink[l],
             'da_lq1': da_lq1[l], 'da_lk1': da_lk1[l], 'da_lq2': da_lq2[l], 'da_lk2': da_lk2[l],
             'da_norm_g': da_norm_g[l], 'w_br_a': w_br_a[l], 'w_br_b': w_br_b[l], 'w_br_c': w_br_c[l],
             'w_out': w_out[l], 'b_out': b_out[l]}
        q = {'router_w': router_w[l], 'router_b': router_b[l],
             'w_gate': exp_w_gate[l], 'b_gate': exp_b_gate[l], 'w_up': exp_w_up[l], 'b_up': exp_b_up[l],
             'w_down': exp_w_down[l], 'b_down': exp_b_down[l]}
        mod_x = (silu_c @ w_ada[l] + b_ada[l])[:, None, :]
        mod_c = silu_cc @ w_ada[l] + b_ada[l]
        sh1x, sc1x, g1x, sh2x, sc2x, g2x = jnp.split(mod_x, 6, axis=-1)
        sh1c, sc1c, g1c, sh2c, sc2c, g2c = jnp.split(mod_c, 6, axis=-1)
        mx, mc = token_mixing(modulate(layer_norm(x), sh1x, sc1x), modulate(layer_norm(ctx), sh1c, sc1c),
                              p, rope_wa, rope_da, lambda_init(l), with_ctx)
        x = layer_norm_affine(DEEPNORM_ALPHA * x + g1x * mx, ln1_g[l], ln1_b[l])
        hx = modulate(layer_norm(x), sh2x, sc2x).reshape(B * S, D)
        if with_ctx:
            ctx = layer_norm_affine(DEEPNORM_ALPHA * ctx + g1c * mc, ln1_g[l], ln1_b[l])
            hc = modulate(layer_norm(ctx), sh2c, sc2c).reshape(B * C, D)
            f = moe_ffn(jnp.concatenate([hx, hc], axis=0), q)
            fx = f[:B * S]
            ctx = layer_norm_affine(DEEPNORM_ALPHA * ctx + g2c * f[B * S:].reshape(B, C, D), ln2_g[l], ln2_b[l])
        else:
            fx = moe_ffn(hx, q)
        x = layer_norm_affine(DEEPNORM_ALPHA * x + g2x * fx.reshape(B, S, D), ln2_g[l], ln2_b[l])
    return x
```

```python
import functools
import math

import jax
import jax.numpy as jnp
from jax import lax
from jax.experimental import pallas as pl
from jax.experimental.pallas import tpu as pltpu

F32 = jnp.float32
BF16 = jnp.bfloat16

LN_EPS = 1e-5
GRID_W = 64
ROPE_BASE = 10000.0
LRU_C = 8.0
WA_HEAD_DIM = 128
DA_HEAD_DIM = 64
WINDOW = 128
TOP_K = 4
SWIGLU_LIMIT = 7.0
SWIGLU_ALPHA = 1.702
LANES = 128
SUBLANES = 8
NEG = -1e30
VMEM_LIMIT = 56 * 1024 * 1024


def _cp(*sem):
    return pltpu.CompilerParams(dimension_semantics=sem, vmem_limit_bytes=VMEM_LIMIT)


def _sigmoid(v):
    return 1.0 / (1.0 + jnp.exp(-v))


def _ln(v):
    mu = jnp.mean(v, axis=-1, keepdims=True)
    d = v - mu
    var = jnp.mean(d * d, axis=-1, keepdims=True)
    return d * lax.rsqrt(var + LN_EPS)


def _ada_kernel(c_ref, w_ref, b_ref, o_ref):
    c = c_ref[...]
    s = (c * _sigmoid(c)).astype(BF16)
    o_ref[...] = jnp.dot(s, w_ref[...].astype(BF16), preferred_element_type=F32) + b_ref[...]


def _ada(cc, w, b):
    D, N = w.shape
    tn = 512
    return pl.pallas_call(
        _ada_kernel,
        grid=(N // tn,),
        in_specs=[pl.BlockSpec((8, D), lambda j: (0, 0)),
                  pl.BlockSpec((D, tn), lambda j: (0, j)),
                  pl.BlockSpec((1, tn), lambda j: (0, j))],
        out_specs=pl.BlockSpec((8, tn), lambda j: (0, j)),
        out_shape=jax.ShapeDtypeStruct((8, N), F32),
        compiler_params=_cp("arbitrary"),
        name="ada",
    )(cc, w, b.reshape(1, N))


def _mod_spec(D, row_fn, chunk):
    return pl.BlockSpec((None, 1, D), lambda *a: (row_fn(*a), 0, chunk))


def _ln0_kernel(x_ref, c_ref, shx, scx, shc, scc, xo_ref, h_ref, *, nC):
    is_ctx = pl.program_id(1) < nC
    v = jnp.where(is_ctx, c_ref[...], x_ref[...])
    sh = jnp.where(is_ctx, shc[...], shx[...])
    sc = jnp.where(is_ctx, scc[...], scx[...])
    xo_ref[...] = v
    h_ref[...] = (_ln(v) * (1.0 + sc) + sh).astype(BF16)


def _ln0(x2, ctx2, mod3, B, S, C):
    D = x2.shape[1]
    Tb = C + S
    tr = 256
    nC = C // tr
    nS = S // tr
    nT = Tb // tr
    bi = lambda b, j: b
    ci = lambda b, j: B
    return pl.pallas_call(
        functools.partial(_ln0_kernel, nC=nC),
        grid=(B, nT),
        in_specs=[pl.BlockSpec((tr, D), lambda b, j: (b * nS + jnp.maximum(j - nC, 0), 0)),
                  pl.BlockSpec((tr, D), lambda b, j: (b * nC + jnp.minimum(j, nC - 1), 0)),
                  _mod_spec(D, bi, 0), _mod_spec(D, bi, 1), _mod_spec(D, ci, 0), _mod_spec(D, ci, 1)],
        out_specs=[pl.BlockSpec((tr, D), lambda b, j: (b * nT + j, 0)),
                   pl.BlockSpec((tr, D), lambda b, j: (b * nT + j, 0))],
        out_shape=[jax.ShapeDtypeStruct((B * Tb, D), F32), jax.ShapeDtypeStruct((B * Tb, D), BF16)],
        compiler_params=_cp("arbitrary", "arbitrary"),
        name="ln0",
    )(x2, ctx2, mod3, mod3, mod3, mod3)


def _mm_kernel(a_ref, w_ref, o_ref):
    o_ref[...] = jnp.dot(a_ref[...], w_ref[...], preferred_element_type=F32).astype(o_ref.dtype)


def _mm(a, w, tm, tn, out_dtype=BF16):
    M, K = a.shape
    N = w.shape[1]
    return pl.pallas_call(
        _mm_kernel,
        grid=(M // tm, N // tn),
        in_specs=[pl.BlockSpec((tm, K), lambda i, j: (i, 0)),
                  pl.BlockSpec((K, tn), lambda i, j: (0, j))],
        out_specs=pl.BlockSpec((tm, tn), lambda i, j: (i, j)),
        out_shape=jax.ShapeDtypeStruct((M, N), out_dtype),
        compiler_params=_cp("arbitrary", "arbitrary"),
        name="mm",
    )(a, w)


def _rglru_kernel(u_ref, cw_ref, cb_ref, wg_ref, bg_ref, lam_ref, y_ref, h_scr, p_scr, *, C, S):
    T = C + S
    nv = T // SUBLANES
    u = u_ref[...].astype(F32)
    row = lax.broadcasted_iota(jnp.int32, (T, LANES), 0)
    pos = jnp.where(row < C, row, row - C)
    length = jnp.where(row < C, C, S)

    def tap(k):
        if k == 0:
            return u
        r = pltpu.roll(u, (-k) % T, 0)
        ok = (pos + k >= 0) & (pos + k < length)
        return jnp.where(ok, r, 0.0)

    cw = cw_ref[...]
    v = cb_ref[...] + cw[0:1] * tap(-1) + cw[1:2] * u + cw[2:3] * tap(1) + cw[3:4] * tap(2)
    g = jnp.dot(v.astype(BF16), wg_ref[...], preferred_element_type=F32) + bg_ref[...]
    lam = lam_ref[...]
    nl = -lam
    sp = jnp.maximum(nl, 0.0) + jnp.log1p(jnp.exp(-jnp.abs(nl)))

    sub = lax.broadcasted_iota(jnp.int32, (nv, SUBLANES, LANES), 1)
    for d in range(2):
        r = _sigmoid(g[:, (2 * d) * LANES:(2 * d + 1) * LANES])
        i = _sigmoid(g[:, (2 * d + 1) * LANES:(2 * d + 2) * LANES])
        a = jnp.exp((-LRU_C) * sp[:, d * LANES:(d + 1) * LANES] * r)
        b = jnp.sqrt((1.0 - a) * (1.0 + a)) * (i * v)
        a3 = a.reshape(nv, SUBLANES, LANES)
        b3 = b.reshape(nv, SUBLANES, LANES)
        for s in (1, 2, 4):
            if d == 0:
                ok = sub >= s
                shift = s
            else:
                ok = sub < SUBLANES - s
                shift = SUBLANES - s
            a_sh = jnp.where(ok, pltpu.roll(a3, shift, 1), 1.0)
            b_sh = jnp.where(ok, pltpu.roll(b3, shift, 1), 0.0)
            b3 = a3 * b_sh + b3
            a3 = a3 * a_sh
        h_scr[d] = b3.reshape(T, LANES)
        p_scr[d] = a3.reshape(T, LANES)

    nvc = C // SUBLANES

    def step(k, carry):
        cf, cr = carry
        rf = pl.multiple_of(k * SUBLANES, SUBLANES)
        of = h_scr[0, pl.ds(rf, SUBLANES), :] + p_scr[0, pl.ds(rf, SUBLANES), :] * cf
        h_scr[0, pl.ds(rf, SUBLANES), :] = of
        cf = jnp.broadcast_to(of[SUBLANES - 1:SUBLANES, :], (SUBLANES, LANES))
        kr = jnp.where(k < nvc, nvc - 1 - k, nv - 1 - (k - nvc))
        rr = pl.multiple_of(kr * SUBLANES, SUBLANES)
        orv = h_scr[1, pl.ds(rr, SUBLANES), :] + p_scr[1, pl.ds(rr, SUBLANES), :] * cr
        h_scr[1, pl.ds(rr, SUBLANES), :] = orv
        cr = jnp.broadcast_to(orv[0:1, :], (SUBLANES, LANES))
        return cf, cr

    z = jnp.zeros((SUBLANES, LANES), F32)
    lax.fori_loop(0, nv, step, (z, z))
    y_ref[...] = (h_scr[0] + h_scr[1]).astype(BF16)


def _rglru(proj, conv_w, conv_b, wg, bg, lam2, B, S, C, d_rnn):
    Tb = C + S
    nb = d_rnn // LANES
    return pl.pallas_call(
        functools.partial(_rglru_kernel, C=C, S=S),
        grid=(B, nb),
        in_specs=[pl.BlockSpec((Tb, LANES), lambda b, n: (b, n)),
                  pl.BlockSpec((4, LANES), lambda b, n: (0, n)),
                  pl.BlockSpec((1, LANES), lambda b, n: (0, n)),
                  pl.BlockSpec((None, LANES, 4 * LANES), lambda b, n: (n, 0, 0)),
                  pl.BlockSpec((None, 1, 4 * LANES), lambda b, n: (n, 0, 0)),
                  pl.BlockSpec((None, 1, 2 * LANES), lambda b, n: (n, 0, 0))],
        out_specs=pl.BlockSpec((Tb, LANES), lambda b, n: (b, n)),
        out_shape=jax.ShapeDtypeStruct((B * Tb, d_rnn), BF16),
        scratch_shapes=[pltpu.VMEM((2, Tb, LANES), F32), pltpu.VMEM((2, Tb, LANES), F32)],
        compiler_params=_cp("arbitrary", "arbitrary"),
        name="rglru",
    )(proj, conv_w, conv_b, wg, bg, lam2)


def _wattn_kernel(sink_ref, q_ref, kp_ref, kc_ref, kn_ref, kx_ref, vp_ref, vc_ref, vn_ref, vx_ref,
                  cos_ref, sin_ref, o_ref, *, C, Tb, G):
    kvh = pl.program_id(1)
    j = pl.program_id(2)
    nblk = Tb // WINDOW
    scale = WA_HEAD_DIM ** -0.5

    def rope(xv, blk):
        r = pl.multiple_of(blk * WINDOW, WINDOW)
        return xv * cos_ref[pl.ds(r, WINDOW), :] + pltpu.roll(xv, WA_HEAD_DIM // 2, 1) * sin_ref[pl.ds(r, WINDOW), :]

    jp = jnp.maximum(j - 1, 0)
    jn = jnp.minimum(j + 1, nblk - 1)
    kcat = jnp.concatenate([rope(kp_ref[...].astype(F32), jp), rope(kc_ref[...].astype(F32), j),
                            rope(kn_ref[...].astype(F32), jn)], axis=0).astype(BF16)
    vcat = jnp.concatenate([vp_ref[...], vc_ref[...], vn_ref[...]], axis=0)
    kx = kx_ref[...]
    vx = vx_ref[...]
    qs = j * WINDOW + lax.broadcasted_iota(jnp.int32, (WINDOW, 3 * WINDOW), 0)
    ks = (j - 1) * WINDOW + lax.broadcasted_iota(jnp.int32, (WINDOW, 3 * WINDOW), 1)
    valid = (qs >= C) & (ks >= C) & (ks < Tb) & (jnp.abs(ks - qs) <= WINDOW)
    nt = (((1,), (1,)), ((), ()))
    for g in range(G):
        qg = (rope(q_ref[:, g * WA_HEAD_DIM:(g + 1) * WA_HEAD_DIM].astype(F32), j) * scale).astype(BF16)
        sl = lax.dot_general(qg, kcat, nt, preferred_element_type=F32)
        sx = lax.dot_general(qg, kx, nt, preferred_element_type=F32)
        sl = jnp.where(valid, sl, NEG)
        sink = sink_ref[kvh * G + g]
        m = jnp.maximum(jnp.maximum(jnp.max(sl, axis=1, keepdims=True), jnp.max(sx, axis=1, keepdims=True)), sink)
        el = jnp.exp(sl - m)
        ex = jnp.exp(sx - m)
        l = jnp.sum(el, axis=1, keepdims=True) + jnp.sum(ex, axis=1, keepdims=True) + jnp.exp(sink - m)
        o = (jnp.dot(el.astype(BF16), vcat, preferred_element_type=F32)
             + jnp.dot(ex.astype(BF16), vx, preferred_element_type=F32)) / l
        o_ref[:, g * WA_HEAD_DIM:(g + 1) * WA_HEAD_DIM] = o.astype(BF16)


def _wattn(proj, sink, cos, sin, B, S, C, q_off, k_off, v_off, kvh_n, G):
    Tb = C + S
    nblk = Tb // WINDOW
    qw = G * WA_HEAD_DIM
    qc0 = q_off // qw
    kc0 = k_off // WA_HEAD_DIM
    vc0 = v_off // WA_HEAD_DIM
    ncb = Tb // C

    def blk(col0, dj):
        return pl.BlockSpec((WINDOW, WA_HEAD_DIM),
                            lambda b, h, j: (b * nblk + jnp.clip(j + dj, 0, nblk - 1), col0 + h))

    def ctxblk(col0):
        return pl.BlockSpec((C, WA_HEAD_DIM), lambda b, h, j: (b * ncb, col0 + h))

    full = pl.BlockSpec((Tb, WA_HEAD_DIM), lambda b, h, j: (0, 0))
    return pl.pallas_call(
        functools.partial(_wattn_kernel, C=C, Tb=Tb, G=G),
        grid=(B, kvh_n, nblk),
        in_specs=[pl.BlockSpec(memory_space=pltpu.SMEM),
                  pl.BlockSpec((WINDOW, qw), lambda b, h, j: (b * nblk + j, qc0 + h)),
                  blk(kc0, -1), blk(kc0, 0), blk(kc0, 1), ctxblk(kc0),
                  blk(vc0, -1), blk(vc0, 0), blk(vc0, 1), ctxblk(vc0),
                  full, full],
        out_specs=pl.BlockSpec((WINDOW, qw), lambda b, h, j: (b * nblk + j, h)),
        out_shape=jax.ShapeDtypeStruct((B * Tb, kvh_n * qw), BF16),
        compiler_params=_cp("arbitrary", "arbitrary", "arbitrary"),
        name="wattn",
    )(sink, proj, proj, proj, proj, proj, proj, proj, proj, proj, cos, sin)


def _dattn_kernel(q_ref, k_ref, v_ref, cos_ref, sin_ref, g_ref, lqk_ref, o_ref, kr_scr, s_scr,
                  *, C, Tb, tq, kc, lam_init):
    t = pl.program_id(2)
    nchunk = Tb // kc
    half = DA_HEAD_DIM // 2
    lane = lax.broadcasted_iota(jnp.int32, (1, LANES), 1)
    lo = (lane % DA_HEAD_DIM) < half

    def rope(xv, cs, sn):
        rot = jnp.where(lo, pltpu.roll(xv, LANES - half, 1), pltpu.roll(xv, half, 1))
        return xv * cs + rot * sn

    @pl.when(t == 0)
    def _():
        kr_scr[...] = rope(k_ref[...].astype(F32), cos_ref[...], sin_ref[...]).astype(BF16)

    r0 = pl.multiple_of(t * tq, tq)
    qv = rope(q_ref[...].astype(F32), cos_ref[pl.ds(r0, tq), :], sin_ref[pl.ds(r0, tq), :]) * (DA_HEAD_DIM ** -0.5)
    q1 = jnp.where(lane < DA_HEAD_DIM, qv, 0.0).astype(BF16)
    q2 = jnp.where(lane >= DA_HEAD_DIM, qv, 0.0).astype(BF16)
    q_lat = (r0 + lax.broadcasted_iota(jnp.int32, (tq, kc), 0)) >= C
    kcol = lax.broadcasted_iota(jnp.int32, (tq, kc), 1)
    nt = (((1,), (1,)), ((), ()))

    def pass_a(c, carry):
        m1, m2 = carry
        k0 = pl.multiple_of(c * kc, kc)
        kk = kr_scr[pl.ds(k0, kc), :]
        valid = q_lat | ((kcol + k0) < C)
        s1 = jnp.where(valid, lax.dot_general(q1, kk, nt, preferred_element_type=F32), NEG)
        s2 = jnp.where(valid, lax.dot_general(q2, kk, nt, preferred_element_type=F32), NEG)
        s_scr[0, c] = s1
        s_scr[1, c] = s2
        return (jnp.maximum(m1, jnp.max(s1, axis=1, keepdims=True)),
                jnp.maximum(m2, jnp.max(s2, axis=1, keepdims=True)))

    neg = jnp.full((tq, 1), NEG, F32)
    m1, m2 = lax.fori_loop(0, nchunk, pass_a, (neg, neg))

    def pass_b(c, carry):
        l1, l2 = carry
        e1 = jnp.exp(s_scr[0, c] - m1)
        e2 = jnp.exp(s_scr[1, c] - m2)
        s_scr[0, c] = e1
        s_scr[1, c] = e2
        return l1 + jnp.sum(e1, axis=1, keepdims=True), l2 + jnp.sum(e2, axis=1, keepdims=True)

    zero = jnp.zeros((tq, 1), F32)
    l1, l2 = lax.fori_loop(0, nchunk, pass_b, (zero, zero))

    lq = lqk_ref[...]
    lam = (jnp.exp(jnp.sum(lq[0:1] * lq[1:2], axis=1, keepdims=True))
           - jnp.exp(jnp.sum(lq[2:3] * lq[3:4], axis=1, keepdims=True)) + lam_init)
    c1 = 1.0 / l1
    c2 = lam / l2

    def pass_c(c, acc):
        k0 = pl.multiple_of(c * kc, kc)
        w = (s_scr[0, c] * c1 - s_scr[1, c] * c2).astype(BF16)
        return acc + jnp.dot(w, v_ref[pl.ds(k0, kc), :], preferred_element_type=F32)

    o = lax.fori_loop(0, nchunk, pass_c, jnp.zeros((tq, LANES), F32))
    o = o * lax.rsqrt(jnp.mean(o * o, axis=1, keepdims=True) + LN_EPS) * g_ref[...] * (1.0 - lam_init)
    o_ref[...] = o.astype(BF16)


def _dattn(proj, cos, sin, g, lqk, B, S, C, q_off, k_off, v_off, heads, lam_init):
    Tb = C + S
    tq = 256
    kc = 256
    nq = Tb // tq
    qc0, kc0, vc0 = q_off // LANES, k_off // LANES, v_off // LANES
    full = pl.BlockSpec((Tb, LANES), lambda b, h, t: (0, 0))
    return pl.pallas_call(
        functools.partial(_dattn_kernel, C=C, Tb=Tb, tq=tq, kc=kc, lam_init=lam_init),
        grid=(B, heads, nq),
        in_specs=[pl.BlockSpec((tq, LANES), lambda b, h, t: (b * nq + t, qc0 + h)),
                  pl.BlockSpec((Tb, LANES), lambda b, h, t: (b, kc0 + h)),
                  pl.BlockSpec((Tb, LANES), lambda b, h, t: (b, vc0 + h)),
                  full, full,
                  pl.BlockSpec((1, LANES), lambda b, h, t: (0, 0)),
                  pl.BlockSpec((4, DA_HEAD_DIM), lambda b, h, t: (0, 0))],
        out_specs=pl.BlockSpec((tq, LANES), lambda b, h, t: (b * nq + t, h)),
        out_shape=jax.ShapeDtypeStruct((B * Tb, heads * LANES), BF16),
        scratch_shapes=[pltpu.VMEM((Tb, LANES), BF16), pltpu.VMEM((2, Tb // kc, tq, kc), F32)],
        compiler_params=_cp("arbitrary", "arbitrary", "arbitrary"),
        name="dattn",
    )(proj, proj, proj, cos, sin, g, lqk)


def _merge_kernel(ya_ref, yb_ref, yc_ref, wa_ref, wb_ref, wc_ref, ga_ref, gb_ref, gc_ref, o_ref):
    def br(y_ref, w_ref, g_ref):
        return _sigmoid(g_ref[...].astype(F32)) * jnp.dot(y_ref[...], w_ref[...], preferred_element_type=F32)
    o_ref[...] = (br(ya_ref, wa_ref, ga_ref) + br(yb_ref, wb_ref, gb_ref) + br(yc_ref, wc_ref, gc_ref)).astype(BF16)


def _merge(ya, yb, yc, wa, wb, wc, proj, g_off, D):
    R = ya.shape[0]
    tm, tn = 544, 512
    g0 = g_off // tn
    nd = D // tn

    def yspec(y):
        return pl.BlockSpec((tm, y.shape[1]), lambda i, j: (i, 0))

    def wspec(w):
        return pl.BlockSpec((w.shape[0], tn), lambda i, j: (0, j))

    def gspec(k):
        return pl.BlockSpec((tm, tn), lambda i, j: (i, g0 + k * nd + j))

    return pl.pallas_call(
        _merge_kernel,
        grid=(R // tm, nd),
        in_specs=[yspec(ya), yspec(yb), yspec(yc), wspec(wa), wspec(wb), wspec(wc), gspec(0), gspec(1), gspec(2)],
        out_specs=pl.BlockSpec((tm, tn), lambda i, j: (i, j)),
        out_shape=jax.ShapeDtypeStruct((R, D), BF16),
        compiler_params=_cp("arbitrary", "arbitrary"),
        name="merge",
    )(ya, yb, yc, wa, wb, wc, proj, proj, proj)


def _post1_kernel(mo_ref, x_ref, bo_ref, g1x, g1c, sh2x, sh2c, sc2x, sc2c, lg_ref, lb_ref, rw_ref, rb_ref,
                  x1_ref, h2_ref, idx_ref, wt_ref, *, nC, alpha, E):
    is_ctx = pl.program_id(1) < nC
    g1 = jnp.where(is_ctx, g1c[...], g1x[...])
    sh2 = jnp.where(is_ctx, sh2c[...], sh2x[...])
    sc2 = jnp.where(is_ctx, sc2c[...], sc2x[...])
    y = alpha * x_ref[...] + g1 * (mo_ref[...].astype(F32) + bo_ref[...])
    x1 = _ln(y) * lg_ref[...] + lb_ref[...]
    x1_ref[...] = x1
    h2 = _ln(x1) * (1.0 + sc2) + sh2
    h2_ref[...] = h2
    logits = jnp.dot(h2, rw_ref[...], preferred_element_type=F32, precision=lax.Precision.HIGHEST) + rb_ref[...]
    col = lax.broadcasted_iota(jnp.int32, logits.shape, 1)
    lane = lax.broadcasted_iota(jnp.int32, (logits.shape[0], LANES), 1)
    idx_out = jnp.zeros((logits.shape[0], LANES), jnp.int32)
    val_out = jnp.zeros((logits.shape[0], LANES), F32)
    lg = logits
    v0 = None
    den = 0.0
    for k in range(TOP_K):
        m = jnp.max(lg, axis=1, keepdims=True)
        am = jnp.min(jnp.where(lg == m, col, E), axis=1, keepdims=True)
        lg = jnp.where(col == am, -jnp.inf, lg)
        if k == 0:
            v0 = m
        e = jnp.exp(m - v0)
        den = den + e
        idx_out = jnp.where(lane == k, am, idx_out)
        val_out = jnp.where(lane == k, e, val_out)
    idx_ref[...] = idx_out
    wt_ref[...] = val_out / den


def _post1(mo, xall, b_out, mod3, ln_g, ln_b, rw, rb, B, S, C, alpha):
    R, D = xall.shape
    E = rw.shape[1]
    Tb = C + S
    tr = 256
    nC = C // tr
    nT = Tb // tr
    bi = lambda b, j: b
    ci = lambda b, j: B
    row = pl.BlockSpec((tr, D), lambda b, j: (b * nT + j, 0))
    vec = pl.BlockSpec((1, D), lambda b, j: (0, 0))
    small = pl.BlockSpec((tr, LANES), lambda b, j: (b * nT + j, 0))
    return pl.pallas_call(
        functools.partial(_post1_kernel, nC=nC, alpha=alpha, E=E),
        grid=(B, nT),
        in_specs=[row, row, vec,
                  _mod_spec(D, bi, 2), _mod_spec(D, ci, 2), _mod_spec(D, bi, 3), _mod_spec(D, ci, 3),
                  _mod_spec(D, bi, 4), _mod_spec(D, ci, 4), vec, vec,
                  pl.BlockSpec((D, E), lambda b, j: (0, 0)), pl.BlockSpec((1, E), lambda b, j: (0, 0))],
        out_specs=[row, row, small, small],
        out_shape=[jax.ShapeDtypeStruct((R, D), F32), jax.ShapeDtypeStruct((R, D), F32),
                   jax.ShapeDtypeStruct((R, LANES), jnp.int32), jax.ShapeDtypeStruct((R, LANES), F32)],
        compiler_params=_cp("arbitrary", "arbitrary"),
        name="post1",
    )(mo, xall, b_out.reshape(1, D), mod3, mod3, mod3, mod3, mod3, mod3,
      ln_g.reshape(1, D), ln_b.reshape(1, D), rw, rb.reshape(1, E))


def _moe_kernel(te_ref, tj0_ref, tok_ref, nact_ref, h_hbm, wg_ref, bg_ref, wu_ref, bu_ref, wd_ref, bd_ref,
                y_ref, buf, sem, *, tm, n_pairs):
    t = pl.program_id(0)

    @pl.when(t < nact_ref[0])
    def _():
        j0 = tj0_ref[t]

        def issue(i, carry):
            tok = tok_ref[jnp.minimum(j0 + i, n_pairs - 1)]
            pltpu.make_async_copy(h_hbm.at[pl.ds(tok, 1), :], buf.at[pl.ds(i, 1), :], sem).start()
            return carry

        lax.fori_loop(0, tm, issue, 0)
        pltpu.make_async_copy(h_hbm.at[pl.ds(0, tm), :], buf, sem).wait()
        xb = buf[...].astype(BF16)
        gate = jnp.minimum(jnp.dot(xb, wg_ref[...], preferred_element_type=F32) + bg_ref[...], SWIGLU_LIMIT)
        up = jnp.clip(jnp.dot(xb, wu_ref[...], preferred_element_type=F32) + bu_ref[...], -SWIGLU_LIMIT, SWIGLU_LIMIT)
        act = ((up + 1.0) * gate * _sigmoid(SWIGLU_ALPHA * gate)).astype(BF16)
        y_ref[...] = jnp.dot(act, wd_ref[...], preferred_element_type=F32) + bd_ref[...]

    @pl.when(t >= nact_ref[0])
    def _():
        y_ref[...] = jnp.zeros_like(y_ref)


def _moe(h2, te, tj0, tok, nact, wg, bg, wu, bu, wd, bd, tm, n_tiles):
    R, D = h2.shape
    E, _, F = wg.shape
    n_pairs = tok.shape[0]
    em = lambda t, te, tj0, tok, nact: (te[t], 0, 0)
    grid_spec = pltpu.PrefetchScalarGridSpec(
        num_scalar_prefetch=4,
        grid=(n_tiles,),
        in_specs=[pl.BlockSpec(memory_space=pl.ANY),
                  pl.BlockSpec((None, D, F), em), pl.BlockSpec((None, 1, F), em),
                  pl.BlockSpec((None, D, F), em), pl.BlockSpec((None, 1, F), em),
                  pl.BlockSpec((None, F, D), em), pl.BlockSpec((None, 1, D), em)],
        out_specs=pl.BlockSpec((tm, D), lambda t, te, tj0, tok, nact: (t, 0)),
        scratch_shapes=[pltpu.VMEM((tm, D), F32), pltpu.SemaphoreType.DMA(())],
    )
    return pl.pallas_call(
        functools.partial(_moe_kernel, tm=tm, n_pairs=n_pairs),
        grid_spec=grid_spec,
        out_shape=jax.ShapeDtypeStruct((n_tiles * tm, D), F32),
        compiler_params=_cp("arbitrary"),
        name="moe",
    )(te, tj0, tok, nact, h2, wg, bg, wu, bu, wd, bd)


def _dispatch(idx, R, E, tm, n_tiles):
    e_flat = idx.reshape(-1)
    n_pairs = e_flat.shape[0]
    oh = e_flat[:, None] == jnp.arange(E, dtype=jnp.int32)[None, :]
    cs = jnp.cumsum(oh.astype(jnp.int32), axis=0)
    rank = jnp.sum(jnp.where(oh, cs, 0), axis=1) - 1
    counts = cs[-1]
    tiles_e = (counts + tm - 1) // tm
    tile_end = jnp.cumsum(tiles_e)
    tile_start = tile_end - tiles_e
    nact = tile_end[-1]
    cum_start = jnp.cumsum(counts) - counts
    dest = jnp.sum(jnp.where(oh, tile_start[None, :], 0), axis=1) * tm + rank
    tok = jnp.arange(n_pairs, dtype=jnp.int32) // TOP_K
    sorted_tok = jnp.sort(e_flat * R + tok) % R
    tt = jnp.minimum(jnp.arange(n_tiles, dtype=jnp.int32), nact - 1)
    te = jnp.sum((tt[:, None] >= tile_end[None, :]).astype(jnp.int32), axis=1)
    te = jnp.minimum(te, E - 1)
    tj0 = cum_start[te] + (tt - tile_start[te]) * tm
    return (te.astype(jnp.int32), tj0.astype(jnp.int32), sorted_tok.astype(jnp.int32),
            nact.reshape(1).astype(jnp.int32), dest.astype(jnp.int32))


def _combine_kernel(dest_ref, y_hbm, wt_ref, x1_ref, g2x, g2c, sh1x, sh1c, sc1x, sc1c, lg_ref, lb_ref,
                    *rest, tc, nT, j_off, nC, alpha, last):
    if last:
        out_ref, buf, sem = rest
    else:
        xo_ref, h_ref, buf, sem = rest
    b = pl.program_id(0)
    j = pl.program_id(1) + j_off
    row0 = (b * nT + j) * tc

    def issue(i, carry):
        for k in range(TOP_K):
            d = dest_ref[(row0 + i) * TOP_K + k]
            pltpu.make_async_copy(y_hbm.at[pl.ds(d, 1), :], buf.at[k, pl.ds(i, 1), :], sem).start()
        return carry

    lax.fori_loop(0, tc, issue, 0)
    for k in range(TOP_K):
        pltpu.make_async_copy(y_hbm.at[pl.ds(0, tc), :], buf.at[k], sem).wait()
    wt = wt_ref[...]
    f = wt[:, 0:1] * buf[0]
    for k in range(1, TOP_K):
        f = f + wt[:, k:k + 1] * buf[k]
    is_ctx = j < nC
    g2 = jnp.where(is_ctx, g2c[...], g2x[...])
    x2 = _ln(alpha * x1_ref[...] + g2 * f) * lg_ref[...] + lb_ref[...]
    if last:
        out_ref[...] = x2
    else:
        sh1 = jnp.where(is_ctx, sh1c[...], sh1x[...])
        sc1 = jnp.where(is_ctx, sc1c[...], sc1x[...])
        xo_ref[...] = x2
        h_ref[...] = (_ln(x2) * (1.0 + sc1) + sh1).astype(BF16)


def _combine(ys, dest, wts, x1, mod3, mod3_next, ln_g, ln_b, B, S, C, alpha, last):
    R, D = x1.shape
    Tb = C + S
    tc = 128
    nC = C // tc
    nT = Tb // tc
    j_off = nC if last else 0
    nJ = nT - j_off
    bi = lambda b, j, d: b
    ci = lambda b, j, d: B
    row = pl.BlockSpec((tc, D), lambda b, j, d: (b * nT + j + j_off, 0))
    vec = pl.BlockSpec((1, D), lambda b, j, d: (0, 0))
    if last:
        out_specs = pl.BlockSpec((tc, D), lambda b, j, d: (b * nJ + j, 0))
        out_shape = jax.ShapeDtypeStruct((B * S, D), F32)
    else:
        out_specs = [row, row]
        out_shape = [jax.ShapeDtypeStruct((R, D), F32), jax.ShapeDtypeStruct((R, D), BF16)]
    grid_spec = pltpu.PrefetchScalarGridSpec(
        num_scalar_prefetch=1,
        grid=(B, nJ),
        in_specs=[pl.BlockSpec(memory_space=pl.ANY),
                  pl.BlockSpec((tc, LANES), lambda b, j, d: (b * nT + j + j_off, 0)),
                  row,
                  _mod_spec(D, bi, 5), _mod_spec(D, ci, 5), _mod_spec(D, bi, 0), _mod_spec(D, ci, 0),
                  _mod_spec(D, bi, 1), _mod_spec(D, ci, 1), vec, vec],
        out_specs=out_specs,
        scratch_shapes=[pltpu.VMEM((TOP_K, tc, D), F32), pltpu.SemaphoreType.DMA(())],
    )
    return pl.pallas_call(
        functools.partial(_combine_kernel, tc=tc, nT=nT, j_off=j_off, nC=nC, alpha=alpha, last=last),
        grid_spec=grid_spec,
        out_shape=out_shape,
        compiler_params=_cp("arbitrary", "arbitrary"),
        name="combine",
    )(dest, ys, wts, x1, mod3, mod3, mod3_next, mod3_next, mod3_next, mod3_next,
      ln_g.reshape(1, D), ln_b.reshape(1, D))


def _rope_tables(S, C, head_dim):
    rows = S // GRID_W
    pos_row = jnp.repeat(jnp.arange(rows, dtype=jnp.int32), GRID_W).astype(F32)
    pos_col = jnp.tile(jnp.arange(GRID_W, dtype=jnp.int32), rows).astype(F32)
    n = head_dim // 4
    inv = ROPE_BASE ** (-jnp.arange(n, dtype=F32) / n)
    ang = jnp.concatenate([pos_row[:, None] * inv, pos_col[:, None] * inv], -1)
    cos, sin = jnp.cos(ang), jnp.sin(ang)
    reps = LANES // head_dim
    cos_f = jnp.tile(jnp.concatenate([cos, cos], -1), (1, reps))
    sin_f = jnp.tile(jnp.concatenate([-sin, sin], -1), (1, reps))
    cos_f = jnp.concatenate([jnp.ones((C, LANES), F32), cos_f], 0)
    sin_f = jnp.concatenate([jnp.zeros((C, LANES), F32), sin_f], 0)
    return cos_f, sin_f


def kernel(x, c, ctx, c_ctx, w_ada, b_ada, w_in, conv_w, conv_b, lru_wr, lru_br, lru_wi, lru_bi, lru_lambda, wa_sink, da_lq1, da_lk1, da_lq2, da_lk2, da_norm_g, w_br_a, w_br_b, w_br_c, w_out, b_out, ln1_g, ln1_b, router_w, router_b, exp_w_gate, exp_b_gate, exp_w_up, exp_b_up, exp_w_down, exp_b_down, ln2_g, ln2_b):
    B, S, D = x.shape
    C = ctx.shape[1]
    depth = w_ada.shape[0]
    Tb = C + S
    R = B * Tb
    d_in = w_in.shape[2]
    d_rnn = w_br_a.shape[1]
    wa_w = w_br_b.shape[1]
    da_w = w_br_c.shape[1]
    kv_w = (d_in - d_rnn - wa_w - 3 * da_w - 3 * D) // 2
    wa_heads = wa_w // WA_HEAD_DIM
    kvh_n = kv_w // WA_HEAD_DIM
    G = wa_heads // kvh_n
    da_heads = da_w // (2 * DA_HEAD_DIM)
    nrb = lru_wr.shape[2]
    E = router_w.shape[2]
    alpha = (2 * depth) ** 0.25
    assert B + 1 <= 8 and lru_wr.shape[3] == LANES and S % GRID_W == 0
    q_off = d_rnn
    k_off = q_off + wa_w
    v_off = k_off + kv_w
    qd_off = v_off + kv_w
    kd_off = qd_off + da_w
    vd_off = kd_off + da_w
    g_off = vd_off + da_w

    cos_wa, sin_wa = _rope_tables(S, C, WA_HEAD_DIM)
    cos_da, sin_da = _rope_tables(S, C, DA_HEAD_DIM)

    cc = jnp.concatenate([c, c_ctx[None, :], jnp.zeros((8 - B - 1, D), F32)], axis=0)
    mods = [_ada(cc, w_ada[l], b_ada[l]).reshape(8, 1, 6 * D) for l in range(depth)]

    xall, h = _ln0(x.reshape(B * S, D), ctx.reshape(B * C, D), mods[0], B, S, C)

    tm_moe = 256
    n_pairs = R * TOP_K
    n_tiles = n_pairs // tm_moe + E

    out = None
    for l in range(depth):
        last = l == depth - 1
        lam_init = 0.8 - 0.6 * math.exp(-0.3 * l)
        proj = _mm(h, w_in[l].astype(BF16), 1088, 512)
        wg = jnp.concatenate([lru_wr[l, 0], lru_wi[l, 0], lru_wr[l, 1], lru_wi[l, 1]], axis=-1).astype(BF16)
        bg = jnp.stack([lru_br[l, 0].reshape(nrb, LANES), lru_bi[l, 0].reshape(nrb, LANES),
                        lru_br[l, 1].reshape(nrb, LANES), lru_bi[l, 1].reshape(nrb, LANES)], axis=1)
        bg = bg.reshape(nrb, 1, 4 * LANES)
        lam2 = jnp.stack([lru_lambda[l, 0].reshape(nrb, LANES), lru_lambda[l, 1].reshape(nrb, LANES)], axis=1)
        lam2 = lam2.reshape(nrb, 1, 2 * LANES)
        ya = _rglru(proj, conv_w[l], conv_b[l].reshape(1, d_rnn), wg, bg, lam2, B, S, C, d_rnn)
        yb = _wattn(proj, wa_sink[l], cos_wa, sin_wa, B, S, C, q_off, k_off, v_off, kvh_n, G)
        lqk = jnp.stack([da_lq1[l], da_lk1[l], da_lq2[l], da_lk2[l]], axis=0)
        yc = _dattn(proj, cos_da, sin_da, da_norm_g[l].reshape(1, LANES), lqk, B, S, C,
                    qd_off, kd_off, vd_off, da_heads, lam_init)
        m = _merge(ya, yb, yc, w_br_a[l].astype(BF16), w_br_b[l].astype(BF16), w_br_c[l].astype(BF16),
                   proj, g_off, D)
        mo = _mm(m, w_out[l].astype(BF16), 1088, 512)
        x1, h2, idx, wts = _post1(mo, xall, b_out[l], mods[l], ln1_g[l], ln1_b[l], router_w[l], router_b[l],
                                  B, S, C, alpha)
        te, tj0, tok, nact, dest = _dispatch(idx[:, :TOP_K], R, E, tm_moe, n_tiles)
        F = exp_w_gate.shape[3]
        ys = _moe(h2, te, tj0, tok, nact,
                  exp_w_gate[l].astype(BF16), exp_b_gate[l].reshape(E, 1, F),
                  exp_w_up[l].astype(BF16), exp_b_up[l].reshape(E, 1, F),
                  exp_w_down[l].astype(BF16), exp_b_down[l].reshape(E, 1, D), tm_moe, n_tiles)
        nxt = mods[l + 1] if not last else mods[l]
        res = _combine(ys, dest, wts, x1, mods[l], nxt, ln2_g[l], ln2_b[l], B, S, C, alpha, last)
        if last:
            out = res.reshape(B, S, D)
        else:
            xall, h = res
    return out
```

```python
import functools
import math

import jax
import jax.numpy as jnp
from jax import lax
from jax.experimental import pallas as pl
from jax.experimental.pallas import tpu as pltpu

F32 = jnp.float32
BF16 = jnp.bfloat16

LN_EPS = 1e-5
GRID_W = 64
ROPE_BASE = 10000.0
LRU_C = 8.0
WA_HEAD_DIM = 128
DA_HEAD_DIM = 64
WINDOW = 128
TOP_K = 4
SWIGLU_LIMIT = 7.0
SWIGLU_ALPHA = 1.702
LANES = 128
SUBLANES = 8
NEG = -1e30
LOG2E = 1.4426950408889634
VMEM_LIMIT = 56 * 1024 * 1024


def _cp(*sem):
    return pltpu.CompilerParams(dimension_semantics=sem, vmem_limit_bytes=VMEM_LIMIT)


def _sigmoid(v):
    return 1.0 / (1.0 + jnp.exp(-v))


def _ln(v):
    mu = jnp.mean(v, axis=-1, keepdims=True)
    d = v - mu
    var = jnp.mean(d * d, axis=-1, keepdims=True)
    return d * lax.rsqrt(var + LN_EPS)


def _ada_kernel(c_ref, w_ref, b_ref, o_ref):
    c = c_ref[...]
    s = (c * _sigmoid(c)).astype(BF16)
    o_ref[...] = jnp.dot(s, w_ref[...].astype(BF16), preferred_element_type=F32) + b_ref[...]


def _ada(cc, w, b, l):
    _, D, N = w.shape
    tn = 512
    return pl.pallas_call(
        _ada_kernel,
        grid=(N // tn,),
        in_specs=[pl.BlockSpec((8, D), lambda j: (0, 0)),
                  pl.BlockSpec((None, D, tn), lambda j: (l, 0, j)),
                  pl.BlockSpec((None, 1, tn), lambda j: (l, 0, j))],
        out_specs=pl.BlockSpec((8, tn), lambda j: (0, j)),
        out_shape=jax.ShapeDtypeStruct((8, N), F32),
        compiler_params=_cp("arbitrary"),
        name="ada",
    )(cc, w, b)


def _cast_kernel(w_ref, o_ref):
    o_ref[...] = w_ref[...].astype(BF16)


def _cast_bf16(w, tr):
    shape = w.shape
    w2 = w.reshape(-1, shape[-1])
    rows, n = w2.shape
    out = pl.pallas_call(
        _cast_kernel,
        grid=(rows // tr,),
        in_specs=[pl.BlockSpec((tr, n), lambda i: (i, 0))],
        out_specs=pl.BlockSpec((tr, n), lambda i: (i, 0)),
        out_shape=jax.ShapeDtypeStruct((rows, n), BF16),
        compiler_params=_cp("arbitrary"),
        name="cast",
    )(w2)
    return out.reshape(shape)


def _mod_spec(D, row_fn, chunk):
    return pl.BlockSpec((None, 1, D), lambda *a: (row_fn(*a), 0, chunk))


def _ln0_kernel(x_ref, c_ref, shx, scx, shc, scc, xo_ref, h_ref, *, nC):
    is_ctx = pl.program_id(1) < nC
    v = jnp.where(is_ctx, c_ref[...], x_ref[...])
    sh = jnp.where(is_ctx, shc[...], shx[...])
    sc = jnp.where(is_ctx, scc[...], scx[...])
    xo_ref[...] = v
    h_ref[...] = (_ln(v) * (1.0 + sc) + sh).astype(BF16)


def _ln0(x2, ctx2, mod3, B, S, C):
    D = x2.shape[1]
    Tb = C + S
    tr = 256
    nC = C // tr
    nS = S // tr
    nT = Tb // tr
    bi = lambda b, j: b
    ci = lambda b, j: B
    return pl.pallas_call(
        functools.partial(_ln0_kernel, nC=nC),
        grid=(B, nT),
        in_specs=[pl.BlockSpec((tr, D), lambda b, j: (b * nS + jnp.maximum(j - nC, 0), 0)),
                  pl.BlockSpec((tr, D), lambda b, j: (b * nC + jnp.minimum(j, nC - 1), 0)),
                  _mod_spec(D, bi, 0), _mod_spec(D, bi, 1), _mod_spec(D, ci, 0), _mod_spec(D, ci, 1)],
        out_specs=[pl.BlockSpec((tr, D), lambda b, j: (b * nT + j, 0)),
                   pl.BlockSpec((tr, D), lambda b, j: (b * nT + j, 0))],
        out_shape=[jax.ShapeDtypeStruct((B * Tb, D), F32), jax.ShapeDtypeStruct((B * Tb, D), BF16)],
        compiler_params=_cp("arbitrary", "arbitrary"),
        name="ln0",
    )(x2, ctx2, mod3, mod3, mod3, mod3)


def _mm_kernel(a_ref, w_ref, o_ref):
    o_ref[...] = jnp.dot(a_ref[...], w_ref[...], preferred_element_type=F32).astype(o_ref.dtype)


def _mm(a, w, l, tm, tn, out_dtype=BF16):
    M, K = a.shape
    N = w.shape[2]
    return pl.pallas_call(
        _mm_kernel,
        grid=(M // tm, N // tn),
        in_specs=[pl.BlockSpec((tm, K), lambda i, j: (i, 0)),
                  pl.BlockSpec((None, K, tn), lambda i, j: (l, 0, j))],
        out_specs=pl.BlockSpec((tm, tn), lambda i, j: (i, j)),
        out_shape=jax.ShapeDtypeStruct((M, N), out_dtype),
        compiler_params=_cp("arbitrary", "arbitrary"),
        name="mm",
    )(a, w)


def _rglru_kernel(u_ref, cw_ref, cb_ref, wg_ref, bg_ref, lam_ref, y_ref, h_scr, p_scr, *, C, S):
    T = C + S
    nv = T // SUBLANES
    u = u_ref[...].astype(F32)
    row = lax.broadcasted_iota(jnp.int32, (T, LANES), 0)
    pos = jnp.where(row < C, row, row - C)
    length = jnp.where(row < C, C, S)

    def tap(k):
        if k == 0:
            return u
        r = pltpu.roll(u, (-k) % T, 0)
        ok = (pos + k >= 0) & (pos + k < length)
        return jnp.where(ok, r, 0.0)

    cw = cw_ref[...]
    v = cb_ref[...] + cw[0:1] * tap(-1) + cw[1:2] * u + cw[2:3] * tap(1) + cw[3:4] * tap(2)
    g = jnp.dot(v.astype(BF16), wg_ref[...], preferred_element_type=F32) + bg_ref[...]
    lam = lam_ref[...]
    nl = -lam
    sp = jnp.maximum(nl, 0.0) + jnp.log1p(jnp.exp(-jnp.abs(nl)))

    sub = lax.broadcasted_iota(jnp.int32, (nv, SUBLANES, LANES), 1)
    for d in range(2):
        r = _sigmoid(g[:, (2 * d) * LANES:(2 * d + 1) * LANES])
        i = _sigmoid(g[:, (2 * d + 1) * LANES:(2 * d + 2) * LANES])
        a = jnp.exp((-LRU_C) * sp[:, d * LANES:(d + 1) * LANES] * r)
        b = jnp.sqrt((1.0 - a) * (1.0 + a)) * (i * v)
        a3 = a.reshape(nv, SUBLANES, LANES)
        b3 = b.reshape(nv, SUBLANES, LANES)
        for s in (1, 2, 4):
            if d == 0:
                ok = sub >= s
                shift = s
            else:
                ok = sub < SUBLANES - s
                shift = SUBLANES - s
            a_sh = jnp.where(ok, pltpu.roll(a3, shift, 1), 1.0)
            b_sh = jnp.where(ok, pltpu.roll(b3, shift, 1), 0.0)
            b3 = a3 * b_sh + b3
            a3 = a3 * a_sh
        h_scr[d] = b3.reshape(T, LANES)
        p_scr[d] = a3.reshape(T, LANES)

    nvc = C // SUBLANES

    def step(k, carry):
        cf, cr = carry
        rf = pl.multiple_of(k * SUBLANES, SUBLANES)
        of = h_scr[0, pl.ds(rf, SUBLANES), :] + p_scr[0, pl.ds(rf, SUBLANES), :] * cf
        h_scr[0, pl.ds(rf, SUBLANES), :] = of
        cf = jnp.broadcast_to(of[SUBLANES - 1:SUBLANES, :], (SUBLANES, LANES))
        kr = jnp.where(k < nvc, nvc - 1 - k, nv - 1 - (k - nvc))
        rr = pl.multiple_of(kr * SUBLANES, SUBLANES)
        orv = h_scr[1, pl.ds(rr, SUBLANES), :] + p_scr[1, pl.ds(rr, SUBLANES), :] * cr
        h_scr[1, pl.ds(rr, SUBLANES), :] = orv
        cr = jnp.broadcast_to(orv[0:1, :], (SUBLANES, LANES))
        return cf, cr

    z = jnp.zeros((SUBLANES, LANES), F32)
    lax.fori_loop(0, nv, step, (z, z))
    y_ref[...] = (h_scr[0] + h_scr[1]).astype(BF16)


def _rglru(proj, conv_w, conv_b, wg, bg, lam2, l, B, S, C, d_rnn):
    Tb = C + S
    nb = d_rnn // LANES
    return pl.pallas_call(
        functools.partial(_rglru_kernel, C=C, S=S),
        grid=(B, nb),
        in_specs=[pl.BlockSpec((Tb, LANES), lambda b, n: (b, n)),
                  pl.BlockSpec((None, 4, LANES), lambda b, n: (l, 0, n)),
                  pl.BlockSpec((None, 1, LANES), lambda b, n: (l, 0, n)),
                  pl.BlockSpec((None, LANES, 4 * LANES), lambda b, n: (n, 0, 0)),
                  pl.BlockSpec((None, 1, 4 * LANES), lambda b, n: (n, 0, 0)),
                  pl.BlockSpec((None, 1, 2 * LANES), lambda b, n: (n, 0, 0))],
        out_specs=pl.BlockSpec((Tb, LANES), lambda b, n: (b, n)),
        out_shape=jax.ShapeDtypeStruct((B * Tb, d_rnn), BF16),
        scratch_shapes=[pltpu.VMEM((2, Tb, LANES), F32), pltpu.VMEM((2, Tb, LANES), F32)],
        compiler_params=_cp("arbitrary", "arbitrary"),
        name="rglru",
    )(proj, conv_w, conv_b, wg, bg, lam2)


def _wattn_kernel(sink_ref, q_ref, kp_ref, kc_ref, kn_ref, kx_ref, vp_ref, vc_ref, vn_ref, vx_ref,
                  cos_ref, sin_ref, o_ref, *, C, Tb, G):
    kvh = pl.program_id(1)
    j = pl.program_id(2)
    nblk = Tb // WINDOW
    scale = WA_HEAD_DIM ** -0.5

    def rope(xv, blk):
        r = pl.multiple_of(blk * WINDOW, WINDOW)
        return xv * cos_ref[pl.ds(r, WINDOW), :] + pltpu.roll(xv, WA_HEAD_DIM // 2, 1) * sin_ref[pl.ds(r, WINDOW), :]

    jp = jnp.maximum(j - 1, 0)
    jn = jnp.minimum(j + 1, nblk - 1)
    kcat = jnp.concatenate([rope(kp_ref[...].astype(F32), jp), rope(kc_ref[...].astype(F32), j),
                            rope(kn_ref[...].astype(F32), jn)], axis=0).astype(BF16)
    vcat = jnp.concatenate([vp_ref[...], vc_ref[...], vn_ref[...]], axis=0)
    kx = kx_ref[...]
    vx = vx_ref[...]
    qs = j * WINDOW + lax.broadcasted_iota(jnp.int32, (WINDOW, 3 * WINDOW), 0)
    ks = (j - 1) * WINDOW + lax.broadcasted_iota(jnp.int32, (WINDOW, 3 * WINDOW), 1)
    valid = (qs >= C) & (ks >= C) & (ks < Tb) & (jnp.abs(ks - qs) <= WINDOW)
    nt = (((1,), (1,)), ((), ()))
    for g in range(G):
        qg = (rope(q_ref[:, g * WA_HEAD_DIM:(g + 1) * WA_HEAD_DIM].astype(F32), j) * scale).astype(BF16)
        sl = lax.dot_general(qg, kcat, nt, preferred_element_type=F32)
        sx = lax.dot_general(qg, kx, nt, preferred_element_type=F32)
        sl = jnp.where(valid, sl, NEG)
        sink = sink_ref[kvh * G + g]
        m = jnp.maximum(jnp.maximum(jnp.max(sl, axis=1, keepdims=True), jnp.max(sx, axis=1, keepdims=True)), sink)
        el = jnp.exp(sl - m)
        ex = jnp.exp(sx - m)
        l = jnp.sum(el, axis=1, keepdims=True) + jnp.sum(ex, axis=1, keepdims=True) + jnp.exp(sink - m)
        o = (jnp.dot(el.astype(BF16), vcat, preferred_element_type=F32)
             + jnp.dot(ex.astype(BF16), vx, preferred_element_type=F32)) / l
        o_ref[:, g * WA_HEAD_DIM:(g + 1) * WA_HEAD_DIM] = o.astype(BF16)


def _wattn(proj, sink, cos, sin, B, S, C, q_off, k_off, v_off, kvh_n, G):
    Tb = C + S
    nblk = Tb // WINDOW
    qw = G * WA_HEAD_DIM
    qc0 = q_off // qw
    kc0 = k_off // WA_HEAD_DIM
    vc0 = v_off // WA_HEAD_DIM
    ncb = Tb // C

    def blk(col0, dj):
        return pl.BlockSpec((WINDOW, WA_HEAD_DIM),
                            lambda b, h, j: (b * nblk + jnp.clip(j + dj, 0, nblk - 1), col0 + h))

    def ctxblk(col0):
        return pl.BlockSpec((C, WA_HEAD_DIM), lambda b, h, j: (b * ncb, col0 + h))

    full = pl.BlockSpec((Tb, WA_HEAD_DIM), lambda b, h, j: (0, 0))
    return pl.pallas_call(
        functools.partial(_wattn_kernel, C=C, Tb=Tb, G=G),
        grid=(B, kvh_n, nblk),
        in_specs=[pl.BlockSpec(memory_space=pltpu.SMEM),
                  pl.BlockSpec((WINDOW, qw), lambda b, h, j: (b * nblk + j, qc0 + h)),
                  blk(kc0, -1), blk(kc0, 0), blk(kc0, 1), ctxblk(kc0),
                  blk(vc0, -1), blk(vc0, 0), blk(vc0, 1), ctxblk(vc0),
                  full, full],
        out_specs=pl.BlockSpec((WINDOW, qw), lambda b, h, j: (b * nblk + j, h)),
        out_shape=jax.ShapeDtypeStruct((B * Tb, kvh_n * qw), BF16),
        compiler_params=_cp("arbitrary", "arbitrary", "arbitrary"),
        name="wattn",
    )(sink, proj, proj, proj, proj, proj, proj, proj, proj, proj, cos, sin)


def _dattn_kernel(q_ref, k_ref, v_ref, cos_ref, sin_ref, g_ref, lqk_ref, o_ref, kr_scr, vt_scr, s_scr,
                  *, C, Tb, tq, lam_init):
    t = pl.program_id(2)
    half = DA_HEAD_DIM // 2
    lane = lax.broadcasted_iota(jnp.int32, (1, LANES), 1)
    lo = (lane % DA_HEAD_DIM) < half

    def rope(xv, cs, sn):
        rot = jnp.where(lo, pltpu.roll(xv, LANES - half, 1), pltpu.roll(xv, half, 1))
        return xv * cs + rot * sn

    @pl.when(t == 0)
    def _():
        kr_scr[...] = rope(k_ref[...].astype(F32), cos_ref[...], sin_ref[...]).astype(BF16)
        vt_scr[...] = v_ref[...].astype(F32).T.astype(BF16)

    r0 = pl.multiple_of(t * tq, tq)
    qv = rope(q_ref[...].astype(F32), cos_ref[pl.ds(r0, tq), :], sin_ref[pl.ds(r0, tq), :]) * (
        DA_HEAD_DIM ** -0.5 * LOG2E)
    qq = jnp.concatenate([jnp.where(lane < DA_HEAD_DIM, qv, 0.0), jnp.where(lane >= DA_HEAD_DIM, qv, 0.0)],
                         axis=0).astype(BF16)
    nt = (((1,), (1,)), ((), ()))
    lq = lqk_ref[...]
    lam = (jnp.exp(jnp.sum(lq[0:1] * lq[1:2], axis=1, keepdims=True))
           - jnp.exp(jnp.sum(lq[2:3] * lq[3:4], axis=1, keepdims=True)) + lam_init)

    def attend(nk):
        s = lax.dot_general(kr_scr[0:nk, :], qq, nt, preferred_element_type=F32)
        s_scr[0:nk, :] = s
        m = jnp.max(s, axis=0, keepdims=True)
        e = jnp.exp2(s_scr[0:nk, :] - m)
        l = jnp.sum(e, axis=0, keepdims=True)
        ot = jnp.dot(vt_scr[:, 0:nk], e.astype(BF16), preferred_element_type=F32)
        o = (ot[:, 0:tq] * (1.0 / l[:, 0:tq]) - ot[:, tq:2 * tq] * (lam / l[:, tq:2 * tq])).T
        o = o * lax.rsqrt(jnp.mean(o * o, axis=1, keepdims=True) + LN_EPS) * g_ref[...] * (1.0 - lam_init)
        o_ref[...] = o.astype(BF16)

    n_ctx_tiles = C // tq

    @pl.when(t < n_ctx_tiles)
    def _():
        attend(C)

    @pl.when(t >= n_ctx_tiles)
    def _():
        attend(Tb)


def _dattn(proj, cos, sin, g, lqk, B, S, C, q_off, k_off, v_off, heads, lam_init):
    Tb = C + S
    tq = 256
    assert C % tq == 0
    nq = Tb // tq
    qc0, kc0, vc0 = q_off // LANES, k_off // LANES, v_off // LANES
    full = pl.BlockSpec((Tb, LANES), lambda b, h, t: (0, 0))
    return pl.pallas_call(
        functools.partial(_dattn_kernel, C=C, Tb=Tb, tq=tq, lam_init=lam_init),
        grid=(B, heads, nq),
        in_specs=[pl.BlockSpec((tq, LANES), lambda b, h, t: (b * nq + t, qc0 + h)),
                  pl.BlockSpec((Tb, LANES), lambda b, h, t: (b, kc0 + h)),
                  pl.BlockSpec((Tb, LANES), lambda b, h, t: (b, vc0 + h)),
                  full, full,
                  pl.BlockSpec((1, LANES), lambda b, h, t: (0, 0)),
                  pl.BlockSpec((4, DA_HEAD_DIM), lambda b, h, t: (0, 0))],
        out_specs=pl.BlockSpec((tq, LANES), lambda b, h, t: (b * nq + t, h)),
        out_shape=jax.ShapeDtypeStruct((B * Tb, heads * LANES), BF16),
        scratch_shapes=[pltpu.VMEM((Tb, LANES), BF16), pltpu.VMEM((LANES, Tb), BF16),
                        pltpu.VMEM((Tb, 2 * tq), F32)],
        compiler_params=_cp("arbitrary", "arbitrary", "arbitrary"),
        name="dattn",
    )(proj, proj, proj, cos, sin, g, lqk)


def _merge_kernel(ya_ref, yb_ref, yc_ref, wa_ref, wb_ref, wc_ref, ga_ref, gb_ref, gc_ref, o_ref):
    def br(y_ref, w_ref, g_ref):
        return _sigmoid(g_ref[...].astype(F32)) * jnp.dot(y_ref[...], w_ref[...], preferred_element_type=F32)
    o_ref[...] = (br(ya_ref, wa_ref, ga_ref) + br(yb_ref, wb_ref, gb_ref) + br(yc_ref, wc_ref, gc_ref)).astype(BF16)


def _merge(ya, yb, yc, wa, wb, wc, l, proj, g_off, D):
    R = ya.shape[0]
    tm, tn = 544, 512
    g0 = g_off // tn
    nd = D // tn

    def yspec(y):
        return pl.BlockSpec((tm, y.shape[1]), lambda i, j: (i, 0))

    def wspec(w):
        return pl.BlockSpec((None, w.shape[1], tn), lambda i, j: (l, 0, j))

    def gspec(k):
        return pl.BlockSpec((tm, tn), lambda i, j: (i, g0 + k * nd + j))

    return pl.pallas_call(
        _merge_kernel,
        grid=(R // tm, nd),
        in_specs=[yspec(ya), yspec(yb), yspec(yc), wspec(wa), wspec(wb), wspec(wc), gspec(0), gspec(1), gspec(2)],
        out_specs=pl.BlockSpec((tm, tn), lambda i, j: (i, j)),
        out_shape=jax.ShapeDtypeStruct((R, D), BF16),
        compiler_params=_cp("arbitrary", "arbitrary"),
        name="merge",
    )(ya, yb, yc, wa, wb, wc, proj, proj, proj)


def _post1_kernel(mo_ref, x_ref, bo_ref, g1x, g1c, sh2x, sh2c, sc2x, sc2c, lg_ref, lb_ref, rw_ref, rb_ref,
                  x1_ref, h2_ref, idx_ref, wt_ref, *, nC, alpha, E):
    is_ctx = pl.program_id(1) < nC
    g1 = jnp.where(is_ctx, g1c[...], g1x[...])
    sh2 = jnp.where(is_ctx, sh2c[...], sh2x[...])
    sc2 = jnp.where(is_ctx, sc2c[...], sc2x[...])
    y = alpha * x_ref[...] + g1 * (mo_ref[...].astype(F32) + bo_ref[...])
    x1 = _ln(y) * lg_ref[...] + lb_ref[...]
    x1_ref[...] = x1
    h2 = _ln(x1) * (1.0 + sc2) + sh2
    h2_ref[...] = h2
    logits = jnp.dot(h2, rw_ref[...], preferred_element_type=F32, precision=lax.Precision.HIGHEST) + rb_ref[...]
    col = lax.broadcasted_iota(jnp.int32, logits.shape, 1)
    lane = lax.broadcasted_iota(jnp.int32, (logits.shape[0], LANES), 1)
    idx_out = jnp.zeros((logits.shape[0], LANES), jnp.int32)
    val_out = jnp.zeros((logits.shape[0], LANES), F32)
    lg = logits
    v0 = None
    den = 0.0
    for k in range(TOP_K):
        m = jnp.max(lg, axis=1, keepdims=True)
        am = jnp.min(jnp.where(lg == m, col, E), axis=1, keepdims=True)
        lg = jnp.where(col == am, -jnp.inf, lg)
        if k == 0:
            v0 = m
        e = jnp.exp(m - v0)
        den = den + e
        idx_out = jnp.where(lane == k, am, idx_out)
        val_out = jnp.where(lane == k, e, val_out)
    idx_ref[...] = idx_out
    wt_ref[...] = val_out / den


def _post1(mo, xall, b_out, mod3, ln_g, ln_b, rw, rb, B, S, C, alpha):
    R, D = xall.shape
    E = rw.shape[1]
    Tb = C + S
    tr = 256
    nC = C // tr
    nT = Tb // tr
    bi = lambda b, j: b
    ci = lambda b, j: B
    row = pl.BlockSpec((tr, D), lambda b, j: (b * nT + j, 0))
    vec = pl.BlockSpec((1, D), lambda b, j: (0, 0))
    small = pl.BlockSpec((tr, LANES), lambda b, j: (b * nT + j, 0))
    return pl.pallas_call(
        functools.partial(_post1_kernel, nC=nC, alpha=alpha, E=E),
        grid=(B, nT),
        in_specs=[row, row, vec,
                  _mod_spec(D, bi, 2), _mod_spec(D, ci, 2), _mod_spec(D, bi, 3), _mod_spec(D, ci, 3),
                  _mod_spec(D, bi, 4), _mod_spec(D, ci, 4), vec, vec,
                  pl.BlockSpec((D, E), lambda b, j: (0, 0)), pl.BlockSpec((1, E), lambda b, j: (0, 0))],
        out_specs=[row, row, small, small],
        out_shape=[jax.ShapeDtypeStruct((R, D), F32), jax.ShapeDtypeStruct((R, D), F32),
                   jax.ShapeDtypeStruct((R, LANES), jnp.int32), jax.ShapeDtypeStruct((R, LANES), F32)],
        compiler_params=_cp("arbitrary", "arbitrary"),
        name="post1",
    )(mo, xall, b_out.reshape(1, D), mod3, mod3, mod3, mod3, mod3, mod3,
      ln_g.reshape(1, D), ln_b.reshape(1, D), rw, rb.reshape(1, E))


def _moe_kernel(te_ref, tj0_ref, tok_ref, nact_ref, h_hbm, wg_ref, bg_ref, wu_ref, bu_ref, wd_ref, bd_ref,
                y_ref, buf, sem, *, tm, n_pairs):
    t = pl.program_id(0)
    nact = nact_ref[0]

    def gather(tile, slot):
        j0 = tj0_ref[tile]

        def issue(i, carry):
            tok = tok_ref[jnp.minimum(j0 + i, n_pairs - 1)]
            pltpu.make_async_copy(h_hbm.at[pl.ds(tok, 1), :], buf.at[slot, pl.ds(i, 1), :], sem.at[slot]).start()
            return carry

        lax.fori_loop(0, tm, issue, 0, unroll=8)

    @pl.when(t == 0)
    def _():
        gather(0, 0)

    @pl.when(t + 1 < nact)
    def _():
        gather(t + 1, (t + 1) % 2)

    @pl.when(t < nact)
    def _():
        slot = t % 2
        pltpu.make_async_copy(h_hbm.at[pl.ds(0, tm), :], buf.at[slot], sem.at[slot]).wait()
        xb = buf[slot].astype(BF16)
        gate = jnp.minimum(jnp.dot(xb, wg_ref[...], preferred_element_type=F32) + bg_ref[...], SWIGLU_LIMIT)
        up = jnp.clip(jnp.dot(xb, wu_ref[...], preferred_element_type=F32) + bu_ref[...], -SWIGLU_LIMIT, SWIGLU_LIMIT)
        act = ((up + 1.0) * gate * _sigmoid(SWIGLU_ALPHA * gate)).astype(BF16)
        y_ref[...] = jnp.dot(act, wd_ref[...], preferred_element_type=F32) + bd_ref[...]

    @pl.when(t >= nact)
    def _():
        y_ref[...] = jnp.zeros_like(y_ref)


def _moe(h2, te, tj0, tok, nact, wg, bg, wu, bu, wd, bd, l, tm, n_tiles):
    R, D = h2.shape
    F = wg.shape[3]
    n_pairs = tok.shape[0]
    em = lambda t, te, tj0, tok, nact: (l, te[t], 0, 0)
    grid_spec = pltpu.PrefetchScalarGridSpec(
        num_scalar_prefetch=4,
        grid=(n_tiles,),
        in_specs=[pl.BlockSpec(memory_space=pl.ANY),
                  pl.BlockSpec((None, None, D, F), em), pl.BlockSpec((None, None, 1, F), em),
                  pl.BlockSpec((None, None, D, F), em), pl.BlockSpec((None, None, 1, F), em),
                  pl.BlockSpec((None, None, F, D), em), pl.BlockSpec((None, None, 1, D), em)],
        out_specs=pl.BlockSpec((tm, D), lambda t, te, tj0, tok, nact: (t, 0)),
        scratch_shapes=[pltpu.VMEM((2, tm, D), F32), pltpu.SemaphoreType.DMA((2,))],
    )
    return pl.pallas_call(
        functools.partial(_moe_kernel, tm=tm, n_pairs=n_pairs),
        grid_spec=grid_spec,
        out_shape=jax.ShapeDtypeStruct((n_tiles * tm, D), F32),
        compiler_params=_cp("arbitrary"),
        name="moe",
    )(te, tj0, tok, nact, h2, wg, bg, wu, bu, wd, bd)


def _dispatch(idx, R, E, tm, n_tiles):
    e_flat = idx.reshape(-1)
    n_pairs = e_flat.shape[0]
    oh = e_flat[:, None] == jnp.arange(E, dtype=jnp.int32)[None, :]
    cs = jnp.cumsum(oh.astype(jnp.int32), axis=0)
    rank = jnp.sum(jnp.where(oh, cs, 0), axis=1) - 1
    counts = cs[-1]
    tiles_e = (counts + tm - 1) // tm
    tile_end = jnp.cumsum(tiles_e)
    tile_start = tile_end - tiles_e
    nact = tile_end[-1]
    cum_start = jnp.cumsum(counts) - counts
    dest = jnp.sum(jnp.where(oh, tile_start[None, :], 0), axis=1) * tm + rank
    tok = jnp.arange(n_pairs, dtype=jnp.int32) // TOP_K
    sorted_tok = jnp.sort(e_flat * R + tok) % R
    tt = jnp.minimum(jnp.arange(n_tiles, dtype=jnp.int32), nact - 1)
    te = jnp.sum((tt[:, None] >= tile_end[None, :]).astype(jnp.int32), axis=1)
    te = jnp.minimum(te, E - 1)
    tj0 = cum_start[te] + (tt - tile_start[te]) * tm
    return (te.astype(jnp.int32), tj0.astype(jnp.int32), sorted_tok.astype(jnp.int32),
            nact.reshape(1).astype(jnp.int32), dest.astype(jnp.int32))


def _combine_kernel(dest_ref, y_hbm, wt_ref, x1_ref, g2x, g2c, sh1x, sh1c, sc1x, sc1c, lg_ref, lb_ref,
                    *rest, tc, nT, j_off, nC, alpha, last):
    if last:
        out_ref, buf, sem = rest
    else:
        xo_ref, h_ref, buf, sem = rest
    nJ = nT - j_off
    step = pl.program_id(0) * nJ + pl.program_id(1)
    n_steps = pl.num_programs(0) * nJ
    j = pl.program_id(1) + j_off

    def gather(s, slot):
        row0 = ((s // nJ) * nT + (s % nJ) + j_off) * tc

        def issue(i, carry):
            for k in range(TOP_K):
                d = dest_ref[(row0 + i) * TOP_K + k]
                pltpu.make_async_copy(y_hbm.at[pl.ds(d, 1), :], buf.at[slot, k, pl.ds(i, 1), :],
                                      sem.at[slot]).start()
            return carry

        lax.fori_loop(0, tc, issue, 0, unroll=4)

    @pl.when(step == 0)
    def _():
        gather(0, 0)

    @pl.when(step + 1 < n_steps)
    def _():
        gather(step + 1, (step + 1) % 2)

    slot = step % 2
    for k in range(TOP_K):
        pltpu.make_async_copy(y_hbm.at[pl.ds(0, tc), :], buf.at[slot, k], sem.at[slot]).wait()
    wt = wt_ref[...]
    f = wt[:, 0:1] * buf[slot, 0]
    for k in range(1, TOP_K):
        f = f + wt[:, k:k + 1] * buf[slot, k]
    is_ctx = j < nC
    g2 = jnp.where(is_ctx, g2c[...], g2x[...])
    x2 = _ln(alpha * x1_ref[...] + g2 * f) * lg_ref[...] + lb_ref[...]
    if last:
        out_ref[...] = x2
    else:
        sh1 = jnp.where(is_ctx, sh1c[...], sh1x[...])
        sc1 = jnp.where(is_ctx, sc1c[...], sc1x[...])
        xo_ref[...] = x2
        h_ref[...] = (_ln(x2) * (1.0 + sc1) + sh1).astype(BF16)


def _combine(ys, dest, wts, x1, mod3, mod3_next, ln_g, ln_b, B, S, C, alpha, last):
    R, D = x1.shape
    Tb = C + S
    tc = 128
    nC = C // tc
    nT = Tb // tc
    j_off = nC if last else 0
    nJ = nT - j_off
    bi = lambda b, j, d: b
    ci = lambda b, j, d: B
    row = pl.BlockSpec((tc, D), lambda b, j, d: (b * nT + j + j_off, 0))
    vec = pl.BlockSpec((1, D), lambda b, j, d: (0, 0))
    if last:
        out_specs = pl.BlockSpec((tc, D), lambda b, j, d: (b * nJ + j, 0))
        out_shape = jax.ShapeDtypeStruct((B * S, D), F32)
    else:
        out_specs = [row, row]
        out_shape = [jax.ShapeDtypeStruct((R, D), F32), jax.ShapeDtypeStruct((R, D), BF16)]
    grid_spec = pltpu.PrefetchScalarGridSpec(
        num_scalar_prefetch=1,
        grid=(B, nJ),
        in_specs=[pl.BlockSpec(memory_space=pl.ANY),
                  pl.BlockSpec((tc, LANES), lambda b, j, d: (b * nT + j + j_off, 0)),
                  row,
                  _mod_spec(D, bi, 5), _mod_spec(D, ci, 5), _mod_spec(D, bi, 0), _mod_spec(D, ci, 0),
                  _mod_spec(D, bi, 1), _mod_spec(D, ci, 1), vec, vec],
        out_specs=out_specs,
        scratch_shapes=[pltpu.VMEM((2, TOP_K, tc, D), F32), pltpu.SemaphoreType.DMA((2,))],
    )
    return pl.pallas_call(
        functools.partial(_combine_kernel, tc=tc, nT=nT, j_off=j_off, nC=nC, alpha=alpha, last=last),
        grid_spec=grid_spec,
        out_shape=out_shape,
        compiler_params=_cp("arbitrary", "arbitrary"),
        name="combine",
    )(dest, ys, wts, x1, mod3, mod3, mod3_next, mod3_next, mod3_next, mod3_next,
      ln_g.reshape(1, D), ln_b.reshape(1, D))


def _rope_tables(S, C, head_dim):
    rows = S // GRID_W
    pos_row = jnp.repeat(jnp.arange(rows, dtype=jnp.int32), GRID_W).astype(F32)
    pos_col = jnp.tile(jnp.arange(GRID_W, dtype=jnp.int32), rows).astype(F32)
    n = head_dim // 4
    inv = ROPE_BASE ** (-jnp.arange(n, dtype=F32) / n)
    ang = jnp.concatenate([pos_row[:, None] * inv, pos_col[:, None] * inv], -1)
    cos, sin = jnp.cos(ang), jnp.sin(ang)
    reps = LANES // head_dim
    cos_f = jnp.tile(jnp.concatenate([cos, cos], -1), (1, reps))
    sin_f = jnp.tile(jnp.concatenate([-sin, sin], -1), (1, reps))
    cos_f = jnp.concatenate([jnp.ones((C, LANES), F32), cos_f], 0)
    sin_f = jnp.concatenate([jnp.zeros((C, LANES), F32), sin_f], 0)
    return cos_f, sin_f


def kernel(x, c, ctx, c_ctx, w_ada, b_ada, w_in, conv_w, conv_b, lru_wr, lru_br, lru_wi, lru_bi, lru_lambda, wa_sink, da_lq1, da_lk1, da_lq2, da_lk2, da_norm_g, w_br_a, w_br_b, w_br_c, w_out, b_out, ln1_g, ln1_b, router_w, router_b, exp_w_gate, exp_b_gate, exp_w_up, exp_b_up, exp_w_down, exp_b_down, ln2_g, ln2_b):
    B, S, D = x.shape
    C = ctx.shape[1]
    depth = w_ada.shape[0]
    Tb = C + S
    R = B * Tb
    d_in = w_in.shape[2]
    d_rnn = w_br_a.shape[1]
    wa_w = w_br_b.shape[1]
    da_w = w_br_c.shape[1]
    kv_w = (d_in - d_rnn - wa_w - 3 * da_w - 3 * D) // 2
    wa_heads = wa_w // WA_HEAD_DIM
    kvh_n = kv_w // WA_HEAD_DIM
    G = wa_heads // kvh_n
    da_heads = da_w // (2 * DA_HEAD_DIM)
    nrb = lru_wr.shape[2]
    E = router_w.shape[2]
    alpha = (2 * depth) ** 0.25
    assert B + 1 <= 8 and lru_wr.shape[3] == LANES and S % GRID_W == 0
    q_off = d_rnn
    k_off = q_off + wa_w
    v_off = k_off + kv_w
    qd_off = v_off + kv_w
    kd_off = qd_off + da_w
    vd_off = kd_off + da_w
    g_off = vd_off + da_w

    cos_wa, sin_wa = _rope_tables(S, C, WA_HEAD_DIM)
    cos_da, sin_da = _rope_tables(S, C, DA_HEAD_DIM)

    cc = jnp.concatenate([c, c_ctx[None, :], jnp.zeros((8 - B - 1, D), F32)], axis=0)
    b_ada3 = b_ada.reshape(depth, 1, 6 * D)
    mods = [_ada(cc, w_ada, b_ada3, l).reshape(8, 1, 6 * D) for l in range(depth)]

    xall, h = _ln0(x.reshape(B * S, D), ctx.reshape(B * C, D), mods[0], B, S, C)

    w_in_b = _cast_bf16(w_in, 64)
    w_bra_b = _cast_bf16(w_br_a, 256)
    w_brb_b = _cast_bf16(w_br_b, 256)
    w_brc_b = _cast_bf16(w_br_c, 256)
    w_out_b = _cast_bf16(w_out, 256)
    wg_b = _cast_bf16(exp_w_gate, 2048)
    wu_b = _cast_bf16(exp_w_up, 2048)
    wd_b = _cast_bf16(exp_w_down, 256)
    F = exp_w_gate.shape[3]
    bg_e = exp_b_gate.reshape(depth, E, 1, F)
    bu_e = exp_b_up.reshape(depth, E, 1, F)
    bd_e = exp_b_down.reshape(depth, E, 1, D)
    conv_b3 = conv_b.reshape(depth, 1, d_rnn)

    tm_moe = 256
    n_pairs = R * TOP_K
    n_tiles = n_pairs // tm_moe + E

    out = None
    for l in range(depth):
        last = l == depth - 1
        lam_init = 0.8 - 0.6 * math.exp(-0.3 * l)
        proj = _mm(h, w_in_b, l, 1088, 512)
        wg = jnp.concatenate([lru_wr[l, 0], lru_wi[l, 0], lru_wr[l, 1], lru_wi[l, 1]], axis=-1).astype(BF16)
        bg = jnp.stack([lru_br[l, 0].reshape(nrb, LANES), lru_bi[l, 0].reshape(nrb, LANES),
                        lru_br[l, 1].reshape(nrb, LANES), lru_bi[l, 1].reshape(nrb, LANES)], axis=1)
        bg = bg.reshape(nrb, 1, 4 * LANES)
        lam2 = jnp.stack([lru_lambda[l, 0].reshape(nrb, LANES), lru_lambda[l, 1].reshape(nrb, LANES)], axis=1)
        lam2 = lam2.reshape(nrb, 1, 2 * LANES)
        ya = _rglru(proj, conv_w, conv_b3, wg, bg, lam2, l, B, S, C, d_rnn)
        yb = _wattn(proj, wa_sink[l], cos_wa, sin_wa, B, S, C, q_off, k_off, v_off, kvh_n, G)
        lqk = jnp.stack([da_lq1[l], da_lk1[l], da_lq2[l], da_lk2[l]], axis=0)
        yc = _dattn(proj, cos_da, sin_da, da_norm_g[l].reshape(1, LANES), lqk, B, S, C,
                    qd_off, kd_off, vd_off, da_heads, lam_init)
        m = _merge(ya, yb, yc, w_bra_b, w_brb_b, w_brc_b, l, proj, g_off, D)
        mo = _mm(m, w_out_b, l, 1088, 512)
        x1, h2, idx, wts = _post1(mo, xall, b_out[l], mods[l], ln1_g[l], ln1_b[l], router_w[l], router_b[l],
                                  B, S, C, alpha)
        te, tj0, tok, nact, dest = _dispatch(idx[:, :TOP_K], R, E, tm_moe, n_tiles)
        ys = _moe(h2, te, tj0, tok, nact, wg_b, bg_e, wu_b, bu_e, wd_b, bd_e, l, tm_moe, n_tiles)
        nxt = mods[l + 1] if not last else mods[l]
        res = _combine(ys, dest, wts, x1, mods[l], nxt, ln2_g[l], ln2_b[l], B, S, C, alpha, last)
        if last:
            out = res.reshape(B, S, D)
        else:
            xall, h = res
    return out
```

```python
import functools
import math

import jax
import jax.numpy as jnp
from jax import lax
from jax.experimental import pallas as pl
from jax.experimental.pallas import tpu as pltpu

F32 = jnp.float32
BF16 = jnp.bfloat16

LN_EPS = 1e-5
GRID_W = 64
ROPE_BASE = 10000.0
LRU_C = 8.0
WA_HEAD_DIM = 128
DA_HEAD_DIM = 64
WINDOW = 128
TOP_K = 4
SWIGLU_LIMIT = 7.0
SWIGLU_ALPHA = 1.702
LANES = 128
SUBLANES = 8
NEG = -1e30
LOG2E = 1.4426950408889634
VMEM_LIMIT = 56 * 1024 * 1024


def _cp(*sem):
    return pltpu.CompilerParams(dimension_semantics=sem, vmem_limit_bytes=VMEM_LIMIT)


def _sigmoid(v):
    return 1.0 / (1.0 + jnp.exp(-v))


def _ln(v):
    mu = jnp.mean(v, axis=-1, keepdims=True)
    d = v - mu
    var = jnp.mean(d * d, axis=-1, keepdims=True)
    return d * lax.rsqrt(var + LN_EPS)


def _ada_kernel(c_ref, w_ref, b_ref, o_ref):
    c = c_ref[...]
    s = (c * _sigmoid(c)).astype(BF16)
    o_ref[...] = jnp.dot(s, w_ref[...].astype(BF16), preferred_element_type=F32) + b_ref[...]


def _ada(cc, w, b, l):
    _, D, N = w.shape
    tn = 512
    return pl.pallas_call(
        _ada_kernel,
        grid=(N // tn,),
        in_specs=[pl.BlockSpec((8, D), lambda j: (0, 0)),
                  pl.BlockSpec((None, D, tn), lambda j: (l, 0, j)),
                  pl.BlockSpec((None, 1, tn), lambda j: (l, 0, j))],
        out_specs=pl.BlockSpec((8, tn), lambda j: (0, j)),
        out_shape=jax.ShapeDtypeStruct((8, N), F32),
        compiler_params=_cp("arbitrary"),
        name="ada",
    )(cc, w, b)


def _cast_kernel(w_ref, o_ref):
    o_ref[...] = w_ref[...].astype(BF16)


def _cast_bf16(w, tr):
    shape = w.shape
    w2 = w.reshape(-1, shape[-1])
    rows, n = w2.shape
    out = pl.pallas_call(
        _cast_kernel,
        grid=(rows // tr,),
        in_specs=[pl.BlockSpec((tr, n), lambda i: (i, 0))],
        out_specs=pl.BlockSpec((tr, n), lambda i: (i, 0)),
        out_shape=jax.ShapeDtypeStruct((rows, n), BF16),
        compiler_params=_cp("arbitrary"),
        name="cast",
    )(w2)
    return out.reshape(shape)


def _mod_spec(D, row_fn, chunk):
    return pl.BlockSpec((None, 1, D), lambda *a: (row_fn(*a), 0, chunk))


def _ln0_kernel(x_ref, c_ref, shx, scx, shc, scc, xo_ref, h_ref, *, nC):
    is_ctx = pl.program_id(1) < nC
    v = jnp.where(is_ctx, c_ref[...], x_ref[...])
    sh = jnp.where(is_ctx, shc[...], shx[...])
    sc = jnp.where(is_ctx, scc[...], scx[...])
    xo_ref[...] = v
    h_ref[...] = (_ln(v) * (1.0 + sc) + sh).astype(BF16)


def _ln0(x2, ctx2, mod3, B, S, C):
    D = x2.shape[1]
    Tb = C + S
    tr = 256
    nC = C // tr
    nS = S // tr
    nT = Tb // tr
    bi = lambda b, j: b
    ci = lambda b, j: B
    return pl.pallas_call(
        functools.partial(_ln0_kernel, nC=nC),
        grid=(B, nT),
        in_specs=[pl.BlockSpec((tr, D), lambda b, j: (b * nS + jnp.maximum(j - nC, 0), 0)),
                  pl.BlockSpec((tr, D), lambda b, j: (b * nC + jnp.minimum(j, nC - 1), 0)),
                  _mod_spec(D, bi, 0), _mod_spec(D, bi, 1), _mod_spec(D, ci, 0), _mod_spec(D, ci, 1)],
        out_specs=[pl.BlockSpec((tr, D), lambda b, j: (b * nT + j, 0)),
                   pl.BlockSpec((tr, D), lambda b, j: (b * nT + j, 0))],
        out_shape=[jax.ShapeDtypeStruct((B * Tb, D), F32), jax.ShapeDtypeStruct((B * Tb, D), BF16)],
        compiler_params=_cp("arbitrary", "arbitrary"),
        name="ln0",
    )(x2, ctx2, mod3, mod3, mod3, mod3)


def _mm_kernel(a_ref, w_ref, o_ref):
    o_ref[...] = jnp.dot(a_ref[...], w_ref[...], preferred_element_type=F32).astype(o_ref.dtype)


def _mm(a, w, l, tm, tn, out_dtype=BF16):
    M, K = a.shape
    N = w.shape[2]
    return pl.pallas_call(
        _mm_kernel,
        grid=(M // tm, N // tn),
        in_specs=[pl.BlockSpec((tm, K), lambda i, j: (i, 0)),
                  pl.BlockSpec((None, K, tn), lambda i, j: (l, 0, j))],
        out_specs=pl.BlockSpec((tm, tn), lambda i, j: (i, j)),
        out_shape=jax.ShapeDtypeStruct((M, N), out_dtype),
        compiler_params=_cp("arbitrary", "arbitrary"),
        name="mm",
    )(a, w)


def _rglru_kernel(u_ref, cw_ref, cb_ref, wg_ref, bg_ref, lam_ref, y_ref, h_scr, p_scr, *, C, S):
    T = C + S
    nv = T // SUBLANES
    u = u_ref[...].astype(F32)
    row = lax.broadcasted_iota(jnp.int32, (T, LANES), 0)
    pos = jnp.where(row < C, row, row - C)
    length = jnp.where(row < C, C, S)

    def tap(k):
        if k == 0:
            return u
        r = pltpu.roll(u, (-k) % T, 0)
        ok = (pos + k >= 0) & (pos + k < length)
        return jnp.where(ok, r, 0.0)

    cw = cw_ref[...]
    v = cb_ref[...] + cw[0:1] * tap(-1) + cw[1:2] * u + cw[2:3] * tap(1) + cw[3:4] * tap(2)
    g = jnp.dot(v.astype(BF16), wg_ref[...], preferred_element_type=F32) + bg_ref[...]
    lam = lam_ref[...]
    nl = -lam
    sp = jnp.maximum(nl, 0.0) + jnp.log1p(jnp.exp(-jnp.abs(nl)))

    sub = lax.broadcasted_iota(jnp.int32, (nv, SUBLANES, LANES), 1)
    for d in range(2):
        r = _sigmoid(g[:, (2 * d) * LANES:(2 * d + 1) * LANES])
        i = _sigmoid(g[:, (2 * d + 1) * LANES:(2 * d + 2) * LANES])
        a = jnp.exp((-LRU_C) * sp[:, d * LANES:(d + 1) * LANES] * r)
        b = jnp.sqrt((1.0 - a) * (1.0 + a)) * (i * v)
        a3 = a.reshape(nv, SUBLANES, LANES)
        b3 = b.reshape(nv, SUBLANES, LANES)
        for s in (1, 2, 4):
            if d == 0:
                ok = sub >= s
                shift = s
            else:
                ok = sub < SUBLANES - s
                shift = SUBLANES - s
            a_sh = jnp.where(ok, pltpu.roll(a3, shift, 1), 1.0)
            b_sh = jnp.where(ok, pltpu.roll(b3, shift, 1), 0.0)
            b3 = a3 * b_sh + b3
            a3 = a3 * a_sh
        h_scr[d] = b3.reshape(T, LANES)
        p_scr[d] = a3.reshape(T, LANES)

    nvc = C // SUBLANES

    def step(k, carry):
        cf, cr = carry
        rf = pl.multiple_of(k * SUBLANES, SUBLANES)
        of = h_scr[0, pl.ds(rf, SUBLANES), :] + p_scr[0, pl.ds(rf, SUBLANES), :] * cf
        h_scr[0, pl.ds(rf, SUBLANES), :] = of
        cf = jnp.broadcast_to(of[SUBLANES - 1:SUBLANES, :], (SUBLANES, LANES))
        kr = jnp.where(k < nvc, nvc - 1 - k, nv - 1 - (k - nvc))
        rr = pl.multiple_of(kr * SUBLANES, SUBLANES)
        orv = h_scr[1, pl.ds(rr, SUBLANES), :] + p_scr[1, pl.ds(rr, SUBLANES), :] * cr
        h_scr[1, pl.ds(rr, SUBLANES), :] = orv
        cr = jnp.broadcast_to(orv[0:1, :], (SUBLANES, LANES))
        return cf, cr

    z = jnp.zeros((SUBLANES, LANES), F32)
    lax.fori_loop(0, nv, step, (z, z))
    y_ref[...] = (h_scr[0] + h_scr[1]).astype(BF16)


def _rglru(proj, conv_w, conv_b, wg, bg, lam2, l, B, S, C, d_rnn):
    Tb = C + S
    nb = d_rnn // LANES
    return pl.pallas_call(
        functools.partial(_rglru_kernel, C=C, S=S),
        grid=(B, nb),
        in_specs=[pl.BlockSpec((Tb, LANES), lambda b, n: (b, n)),
                  pl.BlockSpec((None, 4, LANES), lambda b, n: (l, 0, n)),
                  pl.BlockSpec((None, 1, LANES), lambda b, n: (l, 0, n)),
                  pl.BlockSpec((None, LANES, 4 * LANES), lambda b, n: (n, 0, 0)),
                  pl.BlockSpec((None, 1, 4 * LANES), lambda b, n: (n, 0, 0)),
                  pl.BlockSpec((None, 1, 2 * LANES), lambda b, n: (n, 0, 0))],
        out_specs=pl.BlockSpec((Tb, LANES), lambda b, n: (b, n)),
        out_shape=jax.ShapeDtypeStruct((B * Tb, d_rnn), BF16),
        scratch_shapes=[pltpu.VMEM((2, Tb, LANES), F32), pltpu.VMEM((2, Tb, LANES), F32)],
        compiler_params=_cp("arbitrary", "arbitrary"),
        name="rglru",
    )(proj, conv_w, conv_b, wg, bg, lam2)


def _wattn_kernel(sink_ref, q_ref, k_ref, v_ref, cos_ref, sin_ref, o_ref, kr_scr, vt_scr, *, C, Tb, G):
    kvh = pl.program_id(1)
    j = pl.program_id(2)
    W = WINDOW
    nblk = Tb // W
    ncb = C // W
    scale = WA_HEAD_DIM ** -0.5
    nt = (((1,), (1,)), ((), ()))

    @pl.when(j == 0)
    def _():
        kf = k_ref[...].astype(F32)
        kr = kf * cos_ref[...] + pltpu.roll(kf, WA_HEAD_DIM // 2, 1) * sin_ref[...]
        zk = jnp.zeros((W, WA_HEAD_DIM), BF16)
        kr_scr[0:W, :] = zk
        kr_scr[W:W + Tb, :] = kr.astype(BF16)
        kr_scr[W + Tb:2 * W + Tb, :] = zk
        vt = v_ref[...].astype(F32).T.astype(BF16)
        vt_scr[0] = zk
        vt_scr[nblk + 1] = zk
        for bk in range(nblk):
            vt_scr[bk + 1] = vt[:, bk * W:(bk + 1) * W]

    r0 = pl.multiple_of(j * W, W)
    cs = cos_ref[pl.ds(r0, W), :]
    sn = sin_ref[pl.ds(r0, W), :]
    qs = []
    for g in range(G):
        qg = q_ref[:, g * WA_HEAD_DIM:(g + 1) * WA_HEAD_DIM].astype(F32)
        qs.append((qg * cs + pltpu.roll(qg, WA_HEAD_DIM // 2, 1) * sn) * scale)
    qq = jnp.concatenate(qs, axis=0).astype(BF16)
    sl = lax.dot_general(kr_scr[pl.ds(r0, 3 * W), :], qq, nt, preferred_element_type=F32)
    sx = lax.dot_general(kr_scr[W:W + C, :], qq, nt, preferred_element_type=F32)
    ki = lax.broadcasted_iota(jnp.int32, (3 * W, W), 0)
    qi = lax.broadcasted_iota(jnp.int32, (3 * W, W), 1)
    lo = C - (j - 1) * W
    hi = Tb - (j - 1) * W
    valid = (ki >= qi) & (ki <= qi + 2 * W) & (ki >= lo) & (ki < hi) & (j >= ncb)
    sl = jnp.concatenate([jnp.where(valid, sl[:, g * W:(g + 1) * W], NEG) for g in range(G)], axis=1)
    lane = lax.broadcasted_iota(jnp.int32, (1, G * W), 1)
    sink = jnp.zeros((1, G * W), F32)
    for g in range(G):
        sink = jnp.where(lane // W == g, sink_ref[kvh * G + g], sink)
    m = jnp.maximum(jnp.maximum(jnp.max(sl, axis=0, keepdims=True), jnp.max(sx, axis=0, keepdims=True)), sink)
    el = jnp.exp(sl - m)
    ex = jnp.exp(sx - m)
    l = jnp.sum(el, axis=0, keepdims=True) + jnp.sum(ex, axis=0, keepdims=True) + jnp.exp(sink - m)
    elb = el.astype(BF16)
    exb = ex.astype(BF16)
    ot = jnp.dot(vt_scr[j], elb[0:W], preferred_element_type=F32)
    ot = ot + jnp.dot(vt_scr[j + 1], elb[W:2 * W], preferred_element_type=F32)
    ot = ot + jnp.dot(vt_scr[j + 2], elb[2 * W:3 * W], preferred_element_type=F32)
    for cb in range(ncb):
        ot = ot + jnp.dot(vt_scr[1 + cb], exb[cb * W:(cb + 1) * W], preferred_element_type=F32)
    ot = ot * (1.0 / l)
    for g in range(G):
        o_ref[:, g * WA_HEAD_DIM:(g + 1) * WA_HEAD_DIM] = ot[:, g * W:(g + 1) * W].T.astype(BF16)


def _wattn(proj, sink, cos, sin, B, S, C, q_off, k_off, v_off, kvh_n, G):
    Tb = C + S
    nblk = Tb // WINDOW
    qw = G * WA_HEAD_DIM
    qc0 = q_off // qw
    kc0 = k_off // WA_HEAD_DIM
    vc0 = v_off // WA_HEAD_DIM
    assert C % WINDOW == 0 and WINDOW == WA_HEAD_DIM == LANES

    def slab(col0):
        return pl.BlockSpec((Tb, WA_HEAD_DIM), lambda b, h, j: (b, col0 + h))

    full = pl.BlockSpec((Tb, WA_HEAD_DIM), lambda b, h, j: (0, 0))
    return pl.pallas_call(
        functools.partial(_wattn_kernel, C=C, Tb=Tb, G=G),
        grid=(B, kvh_n, nblk),
        in_specs=[pl.BlockSpec(memory_space=pltpu.SMEM),
                  pl.BlockSpec((WINDOW, qw), lambda b, h, j: (b * nblk + j, qc0 + h)),
                  slab(kc0), slab(vc0), full, full],
        out_specs=pl.BlockSpec((WINDOW, qw), lambda b, h, j: (b * nblk + j, h)),
        out_shape=jax.ShapeDtypeStruct((B * Tb, kvh_n * qw), BF16),
        scratch_shapes=[pltpu.VMEM((Tb + 2 * WINDOW, WA_HEAD_DIM), BF16),
                        pltpu.VMEM((nblk + 2, WA_HEAD_DIM, WINDOW), BF16)],
        compiler_params=_cp("arbitrary", "arbitrary", "arbitrary"),
        name="wattn",
    )(sink, proj, proj, proj, cos, sin)


def _dattn_kernel(q_ref, k_ref, v_ref, cos_ref, sin_ref, g_ref, lqk_ref, o_ref, kr_scr, vt_scr, s0_scr, s1_scr,
                  *, C, S, tq, lam_init):
    Tb = C + S
    half = DA_HEAD_DIM // 2
    lane = lax.broadcasted_iota(jnp.int32, (1, LANES), 1)
    lo = (lane % DA_HEAD_DIM) < half
    nt = (((1,), (1,)), ((), ()))

    def rope(xv, cs, sn):
        rot = jnp.where(lo, pltpu.roll(xv, LANES - half, 1), pltpu.roll(xv, half, 1))
        return xv * cs + rot * sn

    kr_scr[...] = rope(k_ref[...].astype(F32), cos_ref[...], sin_ref[...]).astype(BF16)
    vt_scr[...] = v_ref[...].astype(F32).T.astype(BF16)
    lq = lqk_ref[...]
    lam = (jnp.exp(jnp.sum(lq[0:1] * lq[1:2], axis=1, keepdims=True))
           - jnp.exp(jnp.sum(lq[2:3] * lq[3:4], axis=1, keepdims=True)) + lam_init)
    gain = g_ref[...] * (1.0 - lam_init)

    def scores(r0, s_scr, nk):
        qv = rope(q_ref[pl.ds(r0, tq), :].astype(F32), cos_ref[pl.ds(r0, tq), :], sin_ref[pl.ds(r0, tq), :]) * (
            DA_HEAD_DIM ** -0.5 * LOG2E)
        qq = jnp.concatenate([jnp.where(lane < DA_HEAD_DIM, qv, 0.0), jnp.where(lane >= DA_HEAD_DIM, qv, 0.0)],
                             axis=0).astype(BF16)
        s = lax.dot_general(kr_scr[0:nk, :], qq, nt, preferred_element_type=F32)
        s_scr[0:nk, :] = s
        return jnp.max(s, axis=0, keepdims=True)

    def finish(r0, s_scr, m, nk):
        e = jnp.exp2(s_scr[0:nk, :] - m)
        l = jnp.sum(e, axis=0, keepdims=True)
        ot = jnp.dot(vt_scr[:, 0:nk], e.astype(BF16), preferred_element_type=F32)
        o = (ot[:, 0:tq] * (1.0 / l[:, 0:tq]) - ot[:, tq:2 * tq] * (lam / l[:, tq:2 * tq])).T
        o = o * lax.rsqrt(jnp.mean(o * o, axis=1, keepdims=True) + LN_EPS) * gain
        o_ref[pl.ds(r0, tq), :] = o.astype(BF16)

    for i in range(C // tq):
        finish(i * tq, s0_scr, scores(i * tq, s0_scr, C), C)

    n_lat = S // tq
    m0 = scores(C, s0_scr, Tb)

    def pair(p, m0):
        r = pl.multiple_of(C + p * (2 * tq), tq)
        m1 = scores(r + tq, s1_scr, Tb)
        finish(r, s0_scr, m0, Tb)
        m0 = scores(r + 2 * tq, s0_scr, Tb)
        finish(r + tq, s1_scr, m1, Tb)
        return m0

    m0 = lax.fori_loop(0, n_lat // 2 - 1, pair, m0)
    r = C + (n_lat - 2) * tq
    m1 = scores(r + tq, s1_scr, Tb)
    finish(r, s0_scr, m0, Tb)
    finish(r + tq, s1_scr, m1, Tb)


def _dattn(proj, cos, sin, g, lqk, B, S, C, q_off, k_off, v_off, heads, lam_init):
    Tb = C + S
    tq = 256
    assert C % tq == 0 and (S // tq) % 2 == 0 and S // tq >= 2
    qc0, kc0, vc0 = q_off // LANES, k_off // LANES, v_off // LANES
    full = pl.BlockSpec((Tb, LANES), lambda b, h: (0, 0))

    def slab(col0):
        return pl.BlockSpec((Tb, LANES), lambda b, h: (b, col0 + h))

    return pl.pallas_call(
        functools.partial(_dattn_kernel, C=C, S=S, tq=tq, lam_init=lam_init),
        grid=(B, heads),
        in_specs=[slab(qc0), slab(kc0), slab(vc0), full, full,
                  pl.BlockSpec((1, LANES), lambda b, h: (0, 0)),
                  pl.BlockSpec((4, DA_HEAD_DIM), lambda b, h: (0, 0))],
        out_specs=pl.BlockSpec((Tb, LANES), lambda b, h: (b, h)),
        out_shape=jax.ShapeDtypeStruct((B * Tb, heads * LANES), BF16),
        scratch_shapes=[pltpu.VMEM((Tb, LANES), BF16), pltpu.VMEM((LANES, Tb), BF16),
                        pltpu.VMEM((Tb, 2 * tq), F32), pltpu.VMEM((Tb, 2 * tq), F32)],
        compiler_params=_cp("arbitrary", "arbitrary"),
        name="dattn",
    )(proj, proj, proj, cos, sin, g, lqk)


def _merge_kernel(ya_ref, yb_ref, yc_ref, wa_ref, wb_ref, wc_ref, ga_ref, gb_ref, gc_ref, o_ref):
    def br(y_ref, w_ref, g_ref):
        return _sigmoid(g_ref[...].astype(F32)) * jnp.dot(y_ref[...], w_ref[...], preferred_element_type=F32)
    o_ref[...] = (br(ya_ref, wa_ref, ga_ref) + br(yb_ref, wb_ref, gb_ref) + br(yc_ref, wc_ref, gc_ref)).astype(BF16)


def _merge(ya, yb, yc, wa, wb, wc, l, proj, g_off, D):
    R = ya.shape[0]
    tm, tn = 544, 512
    g0 = g_off // tn
    nd = D // tn

    def yspec(y):
        return pl.BlockSpec((tm, y.shape[1]), lambda i, j: (i, 0))

    def wspec(w):
        return pl.BlockSpec((None, w.shape[1], tn), lambda i, j: (l, 0, j))

    def gspec(k):
        return pl.BlockSpec((tm, tn), lambda i, j: (i, g0 + k * nd + j))

    return pl.pallas_call(
        _merge_kernel,
        grid=(R // tm, nd),
        in_specs=[yspec(ya), yspec(yb), yspec(yc), wspec(wa), wspec(wb), wspec(wc), gspec(0), gspec(1), gspec(2)],
        out_specs=pl.BlockSpec((tm, tn), lambda i, j: (i, j)),
        out_shape=jax.ShapeDtypeStruct((R, D), BF16),
        compiler_params=_cp("arbitrary", "arbitrary"),
        name="merge",
    )(ya, yb, yc, wa, wb, wc, proj, proj, proj)


def _post1_kernel(mo_ref, x_ref, bo_ref, g1x, g1c, sh2x, sh2c, sc2x, sc2c, lg_ref, lb_ref, rw_ref, rb_ref,
                  x1_ref, h2_ref, idx_ref, wt_ref, *, nC, alpha, E):
    is_ctx = pl.program_id(1) < nC
    g1 = jnp.where(is_ctx, g1c[...], g1x[...])
    sh2 = jnp.where(is_ctx, sh2c[...], sh2x[...])
    sc2 = jnp.where(is_ctx, sc2c[...], sc2x[...])
    y = alpha * x_ref[...] + g1 * (mo_ref[...].astype(F32) + bo_ref[...])
    x1 = _ln(y) * lg_ref[...] + lb_ref[...]
    x1_ref[...] = x1
    h2 = _ln(x1) * (1.0 + sc2) + sh2
    h2_ref[...] = h2
    logits = jnp.dot(h2, rw_ref[...], preferred_element_type=F32, precision=lax.Precision.HIGHEST) + rb_ref[...]
    col = lax.broadcasted_iota(jnp.int32, logits.shape, 1)
    lane = lax.broadcasted_iota(jnp.int32, (logits.shape[0], LANES), 1)
    idx_out = jnp.zeros((logits.shape[0], LANES), jnp.int32)
    val_out = jnp.zeros((logits.shape[0], LANES), F32)
    lg = logits
    v0 = None
    den = 0.0
    for k in range(TOP_K):
        m = jnp.max(lg, axis=1, keepdims=True)
        am = jnp.min(jnp.where(lg == m, col, E), axis=1, keepdims=True)
        lg = jnp.where(col == am, -jnp.inf, lg)
        if k == 0:
            v0 = m
        e = jnp.exp(m - v0)
        den = den + e
        idx_out = jnp.where(lane == k, am, idx_out)
        val_out = jnp.where(lane == k, e, val_out)
    idx_ref[...] = idx_out
    wt_ref[...] = val_out / den


def _post1(mo, xall, b_out, mod3, ln_g, ln_b, rw, rb, B, S, C, alpha):
    R, D = xall.shape
    E = rw.shape[1]
    Tb = C + S
    tr = 256
    nC = C // tr
    nT = Tb // tr
    bi = lambda b, j: b
    ci = lambda b, j: B
    row = pl.BlockSpec((tr, D), lambda b, j: (b * nT + j, 0))
    vec = pl.BlockSpec((1, D), lambda b, j: (0, 0))
    small = pl.BlockSpec((tr, LANES), lambda b, j: (b * nT + j, 0))
    return pl.pallas_call(
        functools.partial(_post1_kernel, nC=nC, alpha=alpha, E=E),
        grid=(B, nT),
        in_specs=[row, row, vec,
                  _mod_spec(D, bi, 2), _mod_spec(D, ci, 2), _mod_spec(D, bi, 3), _mod_spec(D, ci, 3),
                  _mod_spec(D, bi, 4), _mod_spec(D, ci, 4), vec, vec,
                  pl.BlockSpec((D, E), lambda b, j: (0, 0)), pl.BlockSpec((1, E), lambda b, j: (0, 0))],
        out_specs=[row, row, small, small],
        out_shape=[jax.ShapeDtypeStruct((R, D), F32), jax.ShapeDtypeStruct((R, D), F32),
                   jax.ShapeDtypeStruct((R, LANES), jnp.int32), jax.ShapeDtypeStruct((R, LANES), F32)],
        compiler_params=_cp("arbitrary", "arbitrary"),
        name="post1",
    )(mo, xall, b_out.reshape(1, D), mod3, mod3, mod3, mod3, mod3, mod3,
      ln_g.reshape(1, D), ln_b.reshape(1, D), rw, rb.reshape(1, E))


def _moe_kernel(te_ref, tj0_ref, tok_ref, nact_ref, h_hbm, wg_ref, bg_ref, wu_ref, bu_ref, wd_ref, bd_ref,
                y_ref, buf, sem, *, tm):
    t = pl.program_id(0)
    nact = nact_ref[0]

    def row_copy(j0, i, slot):
        return pltpu.make_async_copy(h_hbm.at[pl.ds(tok_ref[j0 + i], 1), :], buf.at[slot, pl.ds(i, 1), :],
                                     sem.at[slot])

    def wait_tile(slot):
        pltpu.make_async_copy(h_hbm.at[pl.ds(0, tm), :], buf.at[slot], sem.at[slot]).wait()

    @pl.when(t == 0)
    def _():
        j0 = tj0_ref[0]

        def issue(i, carry):
            row_copy(j0, i, 0).start()
            return carry

        lax.fori_loop(0, tm, issue, 0, unroll=8)

    @pl.when(t == nact)
    def _():
        wait_tile(nact % 2)

    @pl.when(t < nact)
    def _():
        slot = t % 2
        wait_tile(slot)
        xb = buf[slot].astype(BF16)
        j0n = tj0_ref[jnp.minimum(t + 1, nact - 1)]
        for i in range(tm):
            row_copy(j0n, i, 1 - slot).start()
        gate = jnp.minimum(jnp.dot(xb, wg_ref[...], preferred_element_type=F32) + bg_ref[...], SWIGLU_LIMIT)
        up = jnp.clip(jnp.dot(xb, wu_ref[...], preferred_element_type=F32) + bu_ref[...], -SWIGLU_LIMIT, SWIGLU_LIMIT)
        act = ((up + 1.0) * gate * _sigmoid(SWIGLU_ALPHA * gate)).astype(BF16)
        y_ref[...] = jnp.dot(act, wd_ref[...], preferred_element_type=F32) + bd_ref[...]

    @pl.when(t >= nact)
    def _():
        y_ref[...] = jnp.zeros_like(y_ref)


def _moe(h2, te, tj0, tok, nact, wg, bg, wu, bu, wd, bd, l, tm, n_tiles):
    R, D = h2.shape
    F = wg.shape[3]
    em = lambda t, te, tj0, tok, nact: (l, te[t], 0, 0)
    grid_spec = pltpu.PrefetchScalarGridSpec(
        num_scalar_prefetch=4,
        grid=(n_tiles,),
        in_specs=[pl.BlockSpec(memory_space=pl.ANY),
                  pl.BlockSpec((None, None, D, F), em), pl.BlockSpec((None, None, 1, F), em),
                  pl.BlockSpec((None, None, D, F), em), pl.BlockSpec((None, None, 1, F), em),
                  pl.BlockSpec((None, None, F, D), em), pl.BlockSpec((None, None, 1, D), em)],
        out_specs=pl.BlockSpec((tm, D), lambda t, te, tj0, tok, nact: (t, 0)),
        scratch_shapes=[pltpu.VMEM((2, tm, D), F32), pltpu.SemaphoreType.DMA((2,))],
    )
    return pl.pallas_call(
        functools.partial(_moe_kernel, tm=tm),
        grid_spec=grid_spec,
        out_shape=jax.ShapeDtypeStruct((n_tiles * tm, D), F32),
        compiler_params=_cp("arbitrary"),
        name="moe",
    )(te, tj0, tok, nact, h2, wg, bg, wu, bu, wd, bd)


def _dispatch(idx, R, E, tm, n_tiles):
    e_flat = idx.reshape(-1)
    n_pairs = e_flat.shape[0]
    oh = e_flat[:, None] == jnp.arange(E, dtype=jnp.int32)[None, :]
    cs = jnp.cumsum(oh.astype(jnp.int32), axis=0)
    rank = jnp.sum(jnp.where(oh, cs, 0), axis=1) - 1
    counts = cs[-1]
    tiles_e = (counts + tm - 1) // tm
    tile_end = jnp.cumsum(tiles_e)
    tile_start = tile_end - tiles_e
    nact = tile_end[-1]
    cum_start = jnp.cumsum(counts) - counts
    dest = jnp.sum(jnp.where(oh, tile_start[None, :], 0), axis=1) * tm + rank
    tok = jnp.arange(n_pairs, dtype=jnp.int32) // TOP_K
    sorted_tok = jnp.concatenate([jnp.sort(e_flat * R + tok) % R, jnp.zeros((tm,), jnp.int32)])
    tt = jnp.minimum(jnp.arange(n_tiles, dtype=jnp.int32), nact - 1)
    te = jnp.sum((tt[:, None] >= tile_end[None, :]).astype(jnp.int32), axis=1)
    te = jnp.minimum(te, E - 1)
    tj0 = cum_start[te] + (tt - tile_start[te]) * tm
    return (te.astype(jnp.int32), tj0.astype(jnp.int32), sorted_tok.astype(jnp.int32),
            nact.reshape(1).astype(jnp.int32), dest.astype(jnp.int32))


def _combine_kernel(dest_ref, y_hbm, wt_ref, x1_ref, g2x, g2c, sh1x, sh1c, sc1x, sc1c, lg_ref, lb_ref,
                    *rest, tc, nT, j_off, nC, alpha, last):
    if last:
        out_ref, buf, sem = rest
    else:
        xo_ref, h_ref, buf, sem = rest
    nJ = nT - j_off
    step = pl.program_id(0) * nJ + pl.program_id(1)
    n_steps = pl.num_programs(0) * nJ
    j = pl.program_id(1) + j_off

    def row_copy(p0, i, k, slot):
        return pltpu.make_async_copy(y_hbm.at[pl.ds(dest_ref[p0 + i * TOP_K + k], 1), :],
                                     buf.at[slot, k, pl.ds(i, 1), :], sem.at[slot])

    def pair0(s):
        return ((s // nJ) * nT + (s % nJ) + j_off) * (tc * TOP_K)

    def wait_step(slot):
        for k in range(TOP_K):
            pltpu.make_async_copy(y_hbm.at[pl.ds(0, tc), :], buf.at[slot, k], sem.at[slot]).wait()

    @pl.when(step == 0)
    def _():
        def issue(i, carry):
            for k in range(TOP_K):
                row_copy(pair0(0), i, k, 0).start()
            return carry

        lax.fori_loop(0, tc, issue, 0, unroll=4)

    slot = step % 2
    wait_step(slot)
    wt = wt_ref[...]
    f = wt[:, 0:1] * buf[slot, 0]
    for k in range(1, TOP_K):
        f = f + wt[:, k:k + 1] * buf[slot, k]
    p0n = pair0(jnp.minimum(step + 1, n_steps - 1))
    for i in range(tc):
        for k in range(TOP_K):
            row_copy(p0n, i, k, 1 - slot).start()
    is_ctx = j < nC
    g2 = jnp.where(is_ctx, g2c[...], g2x[...])
    x2 = _ln(alpha * x1_ref[...] + g2 * f) * lg_ref[...] + lb_ref[...]
    if last:
        out_ref[...] = x2
    else:
        sh1 = jnp.where(is_ctx, sh1c[...], sh1x[...])
        sc1 = jnp.where(is_ctx, sc1c[...], sc1x[...])
        xo_ref[...] = x2
        h_ref[...] = (_ln(x2) * (1.0 + sc1) + sh1).astype(BF16)

    @pl.when(step == n_steps - 1)
    def _():
        wait_step(1 - slot)


def _combine(ys, dest, wts, x1, mod3, mod3_next, ln_g, ln_b, B, S, C, alpha, last):
    R, D = x1.shape
    Tb = C + S
    tc = 128
    nC = C // tc
    nT = Tb // tc
    j_off = nC if last else 0
    nJ = nT - j_off
    bi = lambda b, j, d: b
    ci = lambda b, j, d: B
    row = pl.BlockSpec((tc, D), lambda b, j, d: (b * nT + j + j_off, 0))
    vec = pl.BlockSpec((1, D), lambda b, j, d: (0, 0))
    if last:
        out_specs = pl.BlockSpec((tc, D), lambda b, j, d: (b * nJ + j, 0))
        out_shape = jax.ShapeDtypeStruct((B * S, D), F32)
    else:
        out_specs = [row, row]
        out_shape = [jax.ShapeDtypeStruct((R, D), F32), jax.ShapeDtypeStruct((R, D), BF16)]
    grid_spec = pltpu.PrefetchScalarGridSpec(
        num_scalar_prefetch=1,
        grid=(B, nJ),
        in_specs=[pl.BlockSpec(memory_space=pl.ANY),
                  pl.BlockSpec((tc, LANES), lambda b, j, d: (b * nT + j + j_off, 0)),
                  row,
                  _mod_spec(D, bi, 5), _mod_spec(D, ci, 5), _mod_spec(D, bi, 0), _mod_spec(D, ci, 0),
                  _mod_spec(D, bi, 1), _mod_spec(D, ci, 1), vec, vec],
        out_specs=out_specs,
        scratch_shapes=[pltpu.VMEM((2, TOP_K, tc, D), F32), pltpu.SemaphoreType.DMA((2,))],
    )
    return pl.pallas_call(
        functools.partial(_combine_kernel, tc=tc, nT=nT, j_off=j_off, nC=nC, alpha=alpha, last=last),
        grid_spec=grid_spec,
        out_shape=out_shape,
        compiler_params=_cp("arbitrary", "arbitrary"),
        name="combine",
    )(dest, ys, wts, x1, mod3, mod3, mod3_next, mod3_next, mod3_next, mod3_next,
      ln_g.reshape(1, D), ln_b.reshape(1, D))


def _rope_tables(S, C, head_dim):
    rows = S // GRID_W
    pos_row = jnp.repeat(jnp.arange(rows, dtype=jnp.int32), GRID_W).astype(F32)
    pos_col = jnp.tile(jnp.arange(GRID_W, dtype=jnp.int32), rows).astype(F32)
    n = head_dim // 4
    inv = ROPE_BASE ** (-jnp.arange(n, dtype=F32) / n)
    ang = jnp.concatenate([pos_row[:, None] * inv, pos_col[:, None] * inv], -1)
    cos, sin = jnp.cos(ang), jnp.sin(ang)
    reps = LANES // head_dim
    cos_f = jnp.tile(jnp.concatenate([cos, cos], -1), (1, reps))
    sin_f = jnp.tile(jnp.concatenate([-sin, sin], -1), (1, reps))
    cos_f = jnp.concatenate([jnp.ones((C, LANES), F32), cos_f], 0)
    sin_f = jnp.concatenate([jnp.zeros((C, LANES), F32), sin_f], 0)
    return cos_f, sin_f


def kernel(x, c, ctx, c_ctx, w_ada, b_ada, w_in, conv_w, conv_b, lru_wr, lru_br, lru_wi, lru_bi, lru_lambda, wa_sink, da_lq1, da_lk1, da_lq2, da_lk2, da_norm_g, w_br_a, w_br_b, w_br_c, w_out, b_out, ln1_g, ln1_b, router_w, router_b, exp_w_gate, exp_b_gate, exp_w_up, exp_b_up, exp_w_down, exp_b_down, ln2_g, ln2_b):
    B, S, D = x.shape
    C = ctx.shape[1]
    depth = w_ada.shape[0]
    Tb = C + S
    R = B * Tb
    d_in = w_in.shape[2]
    d_rnn = w_br_a.shape[1]
    wa_w = w_br_b.shape[1]
    da_w = w_br_c.shape[1]
    kv_w = (d_in - d_rnn - wa_w - 3 * da_w - 3 * D) // 2
    wa_heads = wa_w // WA_HEAD_DIM
    kvh_n = kv_w // WA_HEAD_DIM
    G = wa_heads // kvh_n
    da_heads = da_w // (2 * DA_HEAD_DIM)
    nrb = lru_wr.shape[2]
    E = router_w.shape[2]
    alpha = (2 * depth) ** 0.25
    assert B + 1 <= 8 and lru_wr.shape[3] == LANES and S % GRID_W == 0
    q_off = d_rnn
    k_off = q_off + wa_w
    v_off = k_off + kv_w
    qd_off = v_off + kv_w
    kd_off = qd_off + da_w
    vd_off = kd_off + da_w
    g_off = vd_off + da_w

    cos_wa, sin_wa = _rope_tables(S, C, WA_HEAD_DIM)
    cos_da, sin_da = _rope_tables(S, C, DA_HEAD_DIM)

    cc = jnp.concatenate([c, c_ctx[None, :], jnp.zeros((8 - B - 1, D), F32)], axis=0)
    b_ada3 = b_ada.reshape(depth, 1, 6 * D)
    mods = [_ada(cc, w_ada, b_ada3, l).reshape(8, 1, 6 * D) for l in range(depth)]

    xall, h = _ln0(x.reshape(B * S, D), ctx.reshape(B * C, D), mods[0], B, S, C)

    w_in_b = _cast_bf16(w_in, 64)
    w_bra_b = _cast_bf16(w_br_a, 256)
    w_brb_b = _cast_bf16(w_br_b, 256)
    w_brc_b = _cast_bf16(w_br_c, 256)
    w_out_b = _cast_bf16(w_out, 256)
    wg_b = _cast_bf16(exp_w_gate, 2048)
    wu_b = _cast_bf16(exp_w_up, 2048)
    wd_b = _cast_bf16(exp_w_down, 256)
    F = exp_w_gate.shape[3]
    bg_e = exp_b_gate.reshape(depth, E, 1, F)
    bu_e = exp_b_up.reshape(depth, E, 1, F)
    bd_e = exp_b_down.reshape(depth, E, 1, D)
    conv_b3 = conv_b.reshape(depth, 1, d_rnn)

    tm_moe = 256
    n_pairs = R * TOP_K
    assert n_pairs % tm_moe == 0
    n_tiles = n_pairs // tm_moe + E

    out = None
    for l in range(depth):
        last = l == depth - 1
        lam_init = 0.8 - 0.6 * math.exp(-0.3 * l)
        proj = _mm(h, w_in_b, l, 1088, 512)
        wg = jnp.concatenate([lru_wr[l, 0], lru_wi[l, 0], lru_wr[l, 1], lru_wi[l, 1]], axis=-1).astype(BF16)
        bg = jnp.stack([lru_br[l, 0].reshape(nrb, LANES), lru_bi[l, 0].reshape(nrb, LANES),
                        lru_br[l, 1].reshape(nrb, LANES), lru_bi[l, 1].reshape(nrb, LANES)], axis=1)
        bg = bg.reshape(nrb, 1, 4 * LANES)
        lam2 = jnp.stack([lru_lambda[l, 0].reshape(nrb, LANES), lru_lambda[l, 1].reshape(nrb, LANES)], axis=1)
        lam2 = lam2.reshape(nrb, 1, 2 * LANES)
        ya = _rglru(proj, conv_w, conv_b3, wg, bg, lam2, l, B, S, C, d_rnn)
        yb = _wattn(proj, wa_sink[l], cos_wa, sin_wa, B, S, C, q_off, k_off, v_off, kvh_n, G)
        lqk = jnp.stack([da_lq1[l], da_lk1[l], da_lq2[l], da_lk2[l]], axis=0)
        yc = _dattn(proj, cos_da, sin_da, da_norm_g[l].reshape(1, LANES), lqk, B, S, C,
                    qd_off, kd_off, vd_off, da_heads, lam_init)
        m = _merge(ya, yb, yc, w_bra_b, w_brb_b, w_brc_b, l, proj, g_off, D)
        mo = _mm(m, w_out_b, l, 1088, 512)
        x1, h2, idx, wts = _post1(mo, xall, b_out[l], mods[l], ln1_g[l], ln1_b[l], router_w[l], router_b[l],
                                  B, S, C, alpha)
        te, tj0, tok, nact, dest = _dispatch(idx[:, :TOP_K], R, E, tm_moe, n_tiles)
        ys = _moe(h2, te, tj0, tok, nact, wg_b, bg_e, wu_b, bu_e, wd_b, bd_e, l, tm_moe, n_tiles)
        nxt = mods[l + 1] if not last else mods[l]
        res = _combine(ys, dest, wts, x1, mods[l], nxt, ln2_g[l], ln2_b[l], B, S, C, alpha, last)
        if last:
            out = res.reshape(B, S, D)
        else:
            xall, h = res
    return out
```

```python
import functools
import math

import jax
import jax.numpy as jnp
from jax import lax
from jax.experimental import pallas as pl
from jax.experimental.pallas import tpu as pltpu

F32 = jnp.float32
BF16 = jnp.bfloat16

LN_EPS = 1e-5
GRID_W = 64
ROPE_BASE = 10000.0
LRU_C = 8.0
WA_HEAD_DIM = 128
DA_HEAD_DIM = 64
WINDOW = 128
TOP_K = 4
SWIGLU_LIMIT = 7.0
SWIGLU_ALPHA = 1.702
LANES = 128
SUBLANES = 8
NEG = -1e30
LOG2E = 1.4426950408889634
VMEM_LIMIT = 56 * 1024 * 1024


def _cp(*sem):
    return pltpu.CompilerParams(dimension_semantics=sem, vmem_limit_bytes=VMEM_LIMIT)


def _sigmoid(v):
    return 1.0 / (1.0 + jnp.exp(-v))


def _ln(v):
    mu = jnp.mean(v, axis=-1, keepdims=True)
    d = v - mu
    var = jnp.mean(d * d, axis=-1, keepdims=True)
    return d * lax.rsqrt(var + LN_EPS)


def _ada_kernel(c_ref, w_ref, b_ref, o_ref):
    c = c_ref[...]
    s = (c * _sigmoid(c)).astype(BF16)
    o_ref[...] = jnp.dot(s, w_ref[...].astype(BF16), preferred_element_type=F32) + b_ref[...]


def _ada(cc, w, b, l):
    _, D, N = w.shape
    tn = 512
    return pl.pallas_call(
        _ada_kernel,
        grid=(N // tn,),
        in_specs=[pl.BlockSpec((8, D), lambda j: (0, 0)),
                  pl.BlockSpec((None, D, tn), lambda j: (l, 0, j)),
                  pl.BlockSpec((None, 1, tn), lambda j: (l, 0, j))],
        out_specs=pl.BlockSpec((8, tn), lambda j: (0, j)),
        out_shape=jax.ShapeDtypeStruct((8, N), F32),
        compiler_params=_cp("arbitrary"),
        name="ada",
    )(cc, w, b)


def _cast_kernel(w_ref, o_ref):
    o_ref[...] = w_ref[...].astype(BF16)


def _cast_bf16(w, tr):
    shape = w.shape
    w2 = w.reshape(-1, shape[-1])
    rows, n = w2.shape
    out = pl.pallas_call(
        _cast_kernel,
        grid=(rows // tr,),
        in_specs=[pl.BlockSpec((tr, n), lambda i: (i, 0))],
        out_specs=pl.BlockSpec((tr, n), lambda i: (i, 0)),
        out_shape=jax.ShapeDtypeStruct((rows, n), BF16),
        compiler_params=_cp("arbitrary"),
        name="cast",
    )(w2)
    return out.reshape(shape)


def _mod_spec(D, row_fn, chunk):
    return pl.BlockSpec((None, 1, D), lambda *a: (row_fn(*a), 0, chunk))


def _ln0_kernel(x_ref, c_ref, shx, scx, shc, scc, xo_ref, h_ref, *, nC):
    is_ctx = pl.program_id(1) < nC
    v = jnp.where(is_ctx, c_ref[...], x_ref[...])
    sh = jnp.where(is_ctx, shc[...], shx[...])
    sc = jnp.where(is_ctx, scc[...], scx[...])
    xo_ref[...] = v
    h_ref[...] = (_ln(v) * (1.0 + sc) + sh).astype(BF16)


def _ln0(x2, ctx2, mod3, B, S, C):
    D = x2.shape[1]
    Tb = C + S
    tr = 256
    nC = C // tr
    nS = S // tr
    nT = Tb // tr
    bi = lambda b, j: b
    ci = lambda b, j: B
    return pl.pallas_call(
        functools.partial(_ln0_kernel, nC=nC),
        grid=(B, nT),
        in_specs=[pl.BlockSpec((tr, D), lambda b, j: (b * nS + jnp.maximum(j - nC, 0), 0)),
                  pl.BlockSpec((tr, D), lambda b, j: (b * nC + jnp.minimum(j, nC - 1), 0)),
                  _mod_spec(D, bi, 0), _mod_spec(D, bi, 1), _mod_spec(D, ci, 0), _mod_spec(D, ci, 1)],
        out_specs=[pl.BlockSpec((tr, D), lambda b, j: (b * nT + j, 0)),
                   pl.BlockSpec((tr, D), lambda b, j: (b * nT + j, 0))],
        out_shape=[jax.ShapeDtypeStruct((B * Tb, D), F32), jax.ShapeDtypeStruct((B * Tb, D), BF16)],
        compiler_params=_cp("arbitrary", "arbitrary"),
        name="ln0",
    )(x2, ctx2, mod3, mod3, mod3, mod3)


def _mm_kernel(a_ref, w_ref, o_ref):
    o_ref[...] = jnp.dot(a_ref[...], w_ref[...].astype(BF16), preferred_element_type=F32).astype(o_ref.dtype)


def _mm(a, w, l, tm, tn, out_dtype=BF16):
    M, K = a.shape
    N = w.shape[2]
    return pl.pallas_call(
        _mm_kernel,
        grid=(M // tm, N // tn),
        in_specs=[pl.BlockSpec((tm, K), lambda i, j: (i, 0)),
                  pl.BlockSpec((None, K, tn), lambda i, j: (l, 0, j))],
        out_specs=pl.BlockSpec((tm, tn), lambda i, j: (i, j)),
        out_shape=jax.ShapeDtypeStruct((M, N), out_dtype),
        compiler_params=_cp("arbitrary", "arbitrary"),
        name="mm",
    )(a, w)


def _rglru_kernel(u_ref, cw_ref, cb_ref, wg_ref, bg_ref, lam_ref, y_ref, h_scr, p_scr, *, C, S):
    T = C + S
    nv = T // SUBLANES
    u = u_ref[...].astype(F32)
    row = lax.broadcasted_iota(jnp.int32, (T, LANES), 0)
    pos = jnp.where(row < C, row, row - C)
    length = jnp.where(row < C, C, S)

    def tap(k):
        if k == 0:
            return u
        r = pltpu.roll(u, (-k) % T, 0)
        ok = (pos + k >= 0) & (pos + k < length)
        return jnp.where(ok, r, 0.0)

    cw = cw_ref[...]
    v = cb_ref[...] + cw[0:1] * tap(-1) + cw[1:2] * u + cw[2:3] * tap(1) + cw[3:4] * tap(2)
    g = jnp.dot(v.astype(BF16), wg_ref[...], preferred_element_type=F32) + bg_ref[...]
    lam = lam_ref[...]
    nl = -lam
    sp = jnp.maximum(nl, 0.0) + jnp.log1p(jnp.exp(-jnp.abs(nl)))

    sub = lax.broadcasted_iota(jnp.int32, (nv, SUBLANES, LANES), 1)
    for d in range(2):
        r = _sigmoid(g[:, (2 * d) * LANES:(2 * d + 1) * LANES])
        i = _sigmoid(g[:, (2 * d + 1) * LANES:(2 * d + 2) * LANES])
        a = jnp.exp((-LRU_C) * sp[:, d * LANES:(d + 1) * LANES] * r)
        b = jnp.sqrt((1.0 - a) * (1.0 + a)) * (i * v)
        a3 = a.reshape(nv, SUBLANES, LANES)
        b3 = b.reshape(nv, SUBLANES, LANES)
        for s in (1, 2, 4):
            if d == 0:
                ok = sub >= s
                shift = s
            else:
                ok = sub < SUBLANES - s
                shift = SUBLANES - s
            a_sh = jnp.where(ok, pltpu.roll(a3, shift, 1), 1.0)
            b_sh = jnp.where(ok, pltpu.roll(b3, shift, 1), 0.0)
            b3 = a3 * b_sh + b3
            a3 = a3 * a_sh
        h_scr[d] = b3.reshape(T, LANES)
        p_scr[d] = a3.reshape(T, LANES)

    nvc = C // SUBLANES

    def step(k, carry):
        cf, cr = carry
        rf = pl.multiple_of(k * SUBLANES, SUBLANES)
        of = h_scr[0, pl.ds(rf, SUBLANES), :] + p_scr[0, pl.ds(rf, SUBLANES), :] * cf
        h_scr[0, pl.ds(rf, SUBLANES), :] = of
        cf = jnp.broadcast_to(of[SUBLANES - 1:SUBLANES, :], (SUBLANES, LANES))
        kr = jnp.where(k < nvc, nvc - 1 - k, nv - 1 - (k - nvc))
        rr = pl.multiple_of(kr * SUBLANES, SUBLANES)
        orv = h_scr[1, pl.ds(rr, SUBLANES), :] + p_scr[1, pl.ds(rr, SUBLANES), :] * cr
        h_scr[1, pl.ds(rr, SUBLANES), :] = orv
        cr = jnp.broadcast_to(orv[0:1, :], (SUBLANES, LANES))
        return cf, cr

    z = jnp.zeros((SUBLANES, LANES), F32)
    lax.fori_loop(0, nv, step, (z, z))
    y_ref[...] = (h_scr[0] + h_scr[1]).astype(BF16)


def _rglru(proj, conv_w, conv_b, wg, bg, lam2, l, B, S, C, d_rnn):
    Tb = C + S
    nb = d_rnn // LANES
    return pl.pallas_call(
        functools.partial(_rglru_kernel, C=C, S=S),
        grid=(B, nb),
        in_specs=[pl.BlockSpec((Tb, LANES), lambda b, n: (b, n)),
                  pl.BlockSpec((None, 4, LANES), lambda b, n: (l, 0, n)),
                  pl.BlockSpec((None, 1, LANES), lambda b, n: (l, 0, n)),
                  pl.BlockSpec((None, LANES, 4 * LANES), lambda b, n: (n, 0, 0)),
                  pl.BlockSpec((None, 1, 4 * LANES), lambda b, n: (n, 0, 0)),
                  pl.BlockSpec((None, 1, 2 * LANES), lambda b, n: (n, 0, 0))],
        out_specs=pl.BlockSpec((Tb, LANES), lambda b, n: (b, n)),
        out_shape=jax.ShapeDtypeStruct((B * Tb, d_rnn), BF16),
        scratch_shapes=[pltpu.VMEM((2, Tb, LANES), F32), pltpu.VMEM((2, Tb, LANES), F32)],
        compiler_params=_cp("arbitrary", "arbitrary"),
        name="rglru",
    )(proj, conv_w, conv_b, wg, bg, lam2)


def _wattn_kernel(sink_ref, q_ref, k_ref, v_ref, cos_ref, sin_ref, o_ref, kr_scr, vt_scr, *, C, Tb, G):
    kvh = pl.program_id(1)
    j = pl.program_id(2)
    W = WINDOW
    nblk = Tb // W
    ncb = C // W
    scale = WA_HEAD_DIM ** -0.5
    nt = (((1,), (1,)), ((), ()))

    @pl.when(j == 0)
    def _():
        kf = k_ref[...].astype(F32)
        kr = kf * cos_ref[...] + pltpu.roll(kf, WA_HEAD_DIM // 2, 1) * sin_ref[...]
        zk = jnp.zeros((W, WA_HEAD_DIM), BF16)
        kr_scr[0:W, :] = zk
        kr_scr[W:W + Tb, :] = kr.astype(BF16)
        kr_scr[W + Tb:2 * W + Tb, :] = zk
        vt = v_ref[...].astype(F32).T.astype(BF16)
        vt_scr[0] = zk
        vt_scr[nblk + 1] = zk
        for bk in range(nblk):
            vt_scr[bk + 1] = vt[:, bk * W:(bk + 1) * W]

    r0 = pl.multiple_of(j * W, W)
    cs = cos_ref[pl.ds(r0, W), :]
    sn = sin_ref[pl.ds(r0, W), :]
    qs = []
    for g in range(G):
        qg = q_ref[:, g * WA_HEAD_DIM:(g + 1) * WA_HEAD_DIM].astype(F32)
        qs.append((qg * cs + pltpu.roll(qg, WA_HEAD_DIM // 2, 1) * sn) * scale)
    qq = jnp.concatenate(qs, axis=0).astype(BF16)
    sl = lax.dot_general(kr_scr[pl.ds(r0, 3 * W), :], qq, nt, preferred_element_type=F32)
    sx = lax.dot_general(kr_scr[W:W + C, :], qq, nt, preferred_element_type=F32)
    ki = lax.broadcasted_iota(jnp.int32, (3 * W, W), 0)
    qi = lax.broadcasted_iota(jnp.int32, (3 * W, W), 1)
    lo = C - (j - 1) * W
    hi = Tb - (j - 1) * W
    valid = (ki >= qi) & (ki <= qi + 2 * W) & (ki >= lo) & (ki < hi) & (j >= ncb)
    sl = jnp.concatenate([jnp.where(valid, sl[:, g * W:(g + 1) * W], NEG) for g in range(G)], axis=1)
    lane = lax.broadcasted_iota(jnp.int32, (1, G * W), 1)
    sink = jnp.zeros((1, G * W), F32)
    for g in range(G):
        sink = jnp.where(lane // W == g, sink_ref[kvh * G + g], sink)
    m = jnp.maximum(jnp.maximum(jnp.max(sl, axis=0, keepdims=True), jnp.max(sx, axis=0, keepdims=True)), sink)
    el = jnp.exp(sl - m)
    ex = jnp.exp(sx - m)
    l = jnp.sum(el, axis=0, keepdims=True) + jnp.sum(ex, axis=0, keepdims=True) + jnp.exp(sink - m)
    elb = el.astype(BF16)
    exb = ex.astype(BF16)
    ot = jnp.dot(vt_scr[j], elb[0:W], preferred_element_type=F32)
    ot = ot + jnp.dot(vt_scr[j + 1], elb[W:2 * W], preferred_element_type=F32)
    ot = ot + jnp.dot(vt_scr[j + 2], elb[2 * W:3 * W], preferred_element_type=F32)
    for cb in range(ncb):
        ot = ot + jnp.dot(vt_scr[1 + cb], exb[cb * W:(cb + 1) * W], preferred_element_type=F32)
    ot = ot * (1.0 / l)
    for g in range(G):
        o_ref[:, g * WA_HEAD_DIM:(g + 1) * WA_HEAD_DIM] = ot[:, g * W:(g + 1) * W].T.astype(BF16)


def _wattn(proj, sink, cos, sin, B, S, C, q_off, k_off, v_off, kvh_n, G):
    Tb = C + S
    nblk = Tb // WINDOW
    qw = G * WA_HEAD_DIM
    qc0 = q_off // qw
    kc0 = k_off // WA_HEAD_DIM
    vc0 = v_off // WA_HEAD_DIM
    assert C % WINDOW == 0 and WINDOW == WA_HEAD_DIM == LANES

    def slab(col0):
        return pl.BlockSpec((Tb, WA_HEAD_DIM), lambda b, h, j: (b, col0 + h))

    full = pl.BlockSpec((Tb, WA_HEAD_DIM), lambda b, h, j: (0, 0))
    return pl.pallas_call(
        functools.partial(_wattn_kernel, C=C, Tb=Tb, G=G),
        grid=(B, kvh_n, nblk),
        in_specs=[pl.BlockSpec(memory_space=pltpu.SMEM),
                  pl.BlockSpec((WINDOW, qw), lambda b, h, j: (b * nblk + j, qc0 + h)),
                  slab(kc0), slab(vc0), full, full],
        out_specs=pl.BlockSpec((WINDOW, qw), lambda b, h, j: (b * nblk + j, h)),
        out_shape=jax.ShapeDtypeStruct((B * Tb, kvh_n * qw), BF16),
        scratch_shapes=[pltpu.VMEM((Tb + 2 * WINDOW, WA_HEAD_DIM), BF16),
                        pltpu.VMEM((nblk + 2, WA_HEAD_DIM, WINDOW), BF16)],
        compiler_params=_cp("arbitrary", "arbitrary", "arbitrary"),
        name="wattn",
    )(sink, proj, proj, proj, cos, sin)


def _dattn_kernel(q_ref, k_ref, v_ref, cos_ref, sin_ref, g_ref, lqk_ref, o_ref, kr_scr, vt_scr, s0_scr, s1_scr,
                  *, C, S, tq, lam_init):
    Tb = C + S
    half = DA_HEAD_DIM // 2
    lane = lax.broadcasted_iota(jnp.int32, (1, LANES), 1)
    lo = (lane % DA_HEAD_DIM) < half
    nt = (((1,), (1,)), ((), ()))

    def rope(xv, cs, sn):
        rot = jnp.where(lo, pltpu.roll(xv, LANES - half, 1), pltpu.roll(xv, half, 1))
        return xv * cs + rot * sn

    kr_scr[...] = rope(k_ref[...].astype(F32), cos_ref[...], sin_ref[...]).astype(BF16)
    vt_scr[...] = v_ref[...].astype(F32).T.astype(BF16)
    lq = lqk_ref[...]
    lam = (jnp.exp(jnp.sum(lq[0:1] * lq[1:2], axis=1, keepdims=True))
           - jnp.exp(jnp.sum(lq[2:3] * lq[3:4], axis=1, keepdims=True)) + lam_init)
    gain = g_ref[...] * (1.0 - lam_init)

    def scores(r0, s_scr, nk):
        qv = rope(q_ref[pl.ds(r0, tq), :].astype(F32), cos_ref[pl.ds(r0, tq), :], sin_ref[pl.ds(r0, tq), :]) * (
            DA_HEAD_DIM ** -0.5 * LOG2E)
        qq = jnp.concatenate([jnp.where(lane < DA_HEAD_DIM, qv, 0.0), jnp.where(lane >= DA_HEAD_DIM, qv, 0.0)],
                             axis=0).astype(BF16)
        s = lax.dot_general(kr_scr[0:nk, :], qq, nt, preferred_element_type=F32)
        s_scr[0:nk, :] = s
        return jnp.max(s, axis=0, keepdims=True)

    def finish(r0, s_scr, m, nk):
        e = jnp.exp2(s_scr[0:nk, :] - m)
        l = jnp.sum(e, axis=0, keepdims=True)
        ot = jnp.dot(vt_scr[:, 0:nk], e.astype(BF16), preferred_element_type=F32)
        o = (ot[:, 0:tq] * (1.0 / l[:, 0:tq]) - ot[:, tq:2 * tq] * (lam / l[:, tq:2 * tq])).T
        o = o * lax.rsqrt(jnp.mean(o * o, axis=1, keepdims=True) + LN_EPS) * gain
        o_ref[pl.ds(r0, tq), :] = o.astype(BF16)

    for i in range(C // tq):
        finish(i * tq, s0_scr, scores(i * tq, s0_scr, C), C)

    n_lat = S // tq
    m0 = scores(C, s0_scr, Tb)

    def pair(p, m0):
        r = pl.multiple_of(C + p * (2 * tq), tq)
        m1 = scores(r + tq, s1_scr, Tb)
        finish(r, s0_scr, m0, Tb)
        m0 = scores(r + 2 * tq, s0_scr, Tb)
        finish(r + tq, s1_scr, m1, Tb)
        return m0

    m0 = lax.fori_loop(0, n_lat // 2 - 1, pair, m0)
    r = C + (n_lat - 2) * tq
    m1 = scores(r + tq, s1_scr, Tb)
    finish(r, s0_scr, m0, Tb)
    finish(r + tq, s1_scr, m1, Tb)


def _dattn(proj, cos, sin, g, lqk, B, S, C, q_off, k_off, v_off, heads, lam_init):
    Tb = C + S
    tq = 256
    assert C % tq == 0 and (S // tq) % 2 == 0 and S // tq >= 2
    qc0, kc0, vc0 = q_off // LANES, k_off // LANES, v_off // LANES
    full = pl.BlockSpec((Tb, LANES), lambda b, h: (0, 0))

    def slab(col0):
        return pl.BlockSpec((Tb, LANES), lambda b, h: (b, col0 + h))

    return pl.pallas_call(
        functools.partial(_dattn_kernel, C=C, S=S, tq=tq, lam_init=lam_init),
        grid=(B, heads),
        in_specs=[slab(qc0), slab(kc0), slab(vc0), full, full,
                  pl.BlockSpec((1, LANES), lambda b, h: (0, 0)),
                  pl.BlockSpec((4, DA_HEAD_DIM), lambda b, h: (0, 0))],
        out_specs=pl.BlockSpec((Tb, LANES), lambda b, h: (b, h)),
        out_shape=jax.ShapeDtypeStruct((B * Tb, heads * LANES), BF16),
        scratch_shapes=[pltpu.VMEM((Tb, LANES), BF16), pltpu.VMEM((LANES, Tb), BF16),
                        pltpu.VMEM((Tb, 2 * tq), F32), pltpu.VMEM((Tb, 2 * tq), F32)],
        compiler_params=_cp("arbitrary", "arbitrary"),
        name="dattn",
    )(proj, proj, proj, cos, sin, g, lqk)


def _merge_kernel(ya_ref, yb_ref, yc_ref, wa_ref, wb_ref, wc_ref, ga_ref, gb_ref, gc_ref, o_ref):
    def br(y_ref, w_ref, g_ref):
        return _sigmoid(g_ref[...].astype(F32)) * jnp.dot(y_ref[...], w_ref[...], preferred_element_type=F32)
    o_ref[...] = (br(ya_ref, wa_ref, ga_ref) + br(yb_ref, wb_ref, gb_ref) + br(yc_ref, wc_ref, gc_ref)).astype(BF16)


def _merge(ya, yb, yc, wa, wb, wc, l, proj, g_off, D):
    R = ya.shape[0]
    tm, tn = 544, 512
    g0 = g_off // tn
    nd = D // tn

    def yspec(y):
        return pl.BlockSpec((tm, y.shape[1]), lambda i, j: (i, 0))

    def wspec(w):
        return pl.BlockSpec((None, w.shape[1], tn), lambda i, j: (l, 0, j))

    def gspec(k):
        return pl.BlockSpec((tm, tn), lambda i, j: (i, g0 + k * nd + j))

    return pl.pallas_call(
        _merge_kernel,
        grid=(R // tm, nd),
        in_specs=[yspec(ya), yspec(yb), yspec(yc), wspec(wa), wspec(wb), wspec(wc), gspec(0), gspec(1), gspec(2)],
        out_specs=pl.BlockSpec((tm, tn), lambda i, j: (i, j)),
        out_shape=jax.ShapeDtypeStruct((R, D), BF16),
        compiler_params=_cp("arbitrary", "arbitrary"),
        name="merge",
    )(ya, yb, yc, wa, wb, wc, proj, proj, proj)


def _post1_kernel(mo_ref, x_ref, bo_ref, g1x, g1c, sh2x, sh2c, sc2x, sc2c, lg_ref, lb_ref, rw_ref, rb_ref,
                  x1_ref, h2_ref, idx_ref, wt_ref, *, nC, alpha, E):
    is_ctx = pl.program_id(1) < nC
    g1 = jnp.where(is_ctx, g1c[...], g1x[...])
    sh2 = jnp.where(is_ctx, sh2c[...], sh2x[...])
    sc2 = jnp.where(is_ctx, sc2c[...], sc2x[...])
    y = alpha * x_ref[...] + g1 * (mo_ref[...].astype(F32) + bo_ref[...])
    x1 = _ln(y) * lg_ref[...] + lb_ref[...]
    x1_ref[...] = x1
    h2 = _ln(x1) * (1.0 + sc2) + sh2
    h2_ref[...] = h2
    logits = jnp.dot(h2, rw_ref[...], preferred_element_type=F32, precision=lax.Precision.HIGHEST) + rb_ref[...]
    col = lax.broadcasted_iota(jnp.int32, logits.shape, 1)
    lane = lax.broadcasted_iota(jnp.int32, (logits.shape[0], LANES), 1)
    idx_out = jnp.zeros((logits.shape[0], LANES), jnp.int32)
    val_out = jnp.zeros((logits.shape[0], LANES), F32)
    lg = logits
    v0 = None
    den = 0.0
    for k in range(TOP_K):
        m = jnp.max(lg, axis=1, keepdims=True)
        am = jnp.min(jnp.where(lg == m, col, E), axis=1, keepdims=True)
        lg = jnp.where(col == am, -jnp.inf, lg)
        if k == 0:
            v0 = m
        e = jnp.exp(m - v0)
        den = den + e
        idx_out = jnp.where(lane == k, am, idx_out)
        val_out = jnp.where(lane == k, e, val_out)
    idx_ref[...] = idx_out
    wt_ref[...] = val_out / den


def _post1(mo, xall, b_out, mod3, ln_g, ln_b, rw, rb, B, S, C, alpha):
    R, D = xall.shape
    E = rw.shape[1]
    Tb = C + S
    tr = 256
    nC = C // tr
    nT = Tb // tr
    bi = lambda b, j: b
    ci = lambda b, j: B
    row = pl.BlockSpec((tr, D), lambda b, j: (b * nT + j, 0))
    vec = pl.BlockSpec((1, D), lambda b, j: (0, 0))
    small = pl.BlockSpec((tr, LANES), lambda b, j: (b * nT + j, 0))
    return pl.pallas_call(
        functools.partial(_post1_kernel, nC=nC, alpha=alpha, E=E),
        grid=(B, nT),
        in_specs=[row, row, vec,
                  _mod_spec(D, bi, 2), _mod_spec(D, ci, 2), _mod_spec(D, bi, 3), _mod_spec(D, ci, 3),
                  _mod_spec(D, bi, 4), _mod_spec(D, ci, 4), vec, vec,
                  pl.BlockSpec((D, E), lambda b, j: (0, 0)), pl.BlockSpec((1, E), lambda b, j: (0, 0))],
        out_specs=[row, row, small, small],
        out_shape=[jax.ShapeDtypeStruct((R, D), F32), jax.ShapeDtypeStruct((R, D), F32),
                   jax.ShapeDtypeStruct((R, LANES), jnp.int32), jax.ShapeDtypeStruct((R, LANES), F32)],
        compiler_params=_cp("arbitrary", "arbitrary"),
        name="post1",
    )(mo, xall, b_out.reshape(1, D), mod3, mod3, mod3, mod3, mod3, mod3,
      ln_g.reshape(1, D), ln_b.reshape(1, D), rw, rb.reshape(1, E))


def _moe_kernel(te_ref, tj0_ref, tok_ref, nact_ref, h_hbm, wg_ref, bg_ref, wu_ref, bu_ref, wd_ref, bd_ref,
                y_ref, buf, sem, *, tm):
    t = pl.program_id(0)
    nact = nact_ref[0]

    def row_copy(j0, i, slot):
        return pltpu.make_async_copy(h_hbm.at[pl.ds(tok_ref[j0 + i], 1), :], buf.at[slot, pl.ds(i, 1), :],
                                     sem.at[slot])

    def wait_tile(slot):
        pltpu.make_async_copy(h_hbm.at[pl.ds(0, tm), :], buf.at[slot], sem.at[slot]).wait()

    def gather(tile, slot):
        j0 = tj0_ref[tile]

        def issue(i, carry):
            row_copy(j0, i, slot).start()
            return carry

        lax.fori_loop(0, tm, issue, 0, unroll=8)

    @pl.when(t == 0)
    def _():
        gather(0, 0)

    @pl.when(t + 1 < nact)
    def _():
        gather(t + 1, (t + 1) % 2)

    @pl.when(t < nact)
    def _():
        slot = t % 2
        wait_tile(slot)
        xb = buf[slot].astype(BF16)
        gate = jnp.minimum(jnp.dot(xb, wg_ref[...], preferred_element_type=F32) + bg_ref[...], SWIGLU_LIMIT)
        up = jnp.clip(jnp.dot(xb, wu_ref[...], preferred_element_type=F32) + bu_ref[...], -SWIGLU_LIMIT, SWIGLU_LIMIT)
        act = ((up + 1.0) * gate * _sigmoid(SWIGLU_ALPHA * gate)).astype(BF16)
        y_ref[...] = jnp.dot(act, wd_ref[...], preferred_element_type=F32) + bd_ref[...]

    @pl.when(t >= nact)
    def _():
        y_ref[...] = jnp.zeros_like(y_ref)


def _moe(h2, te, tj0, tok, nact, wg, bg, wu, bu, wd, bd, l, tm, n_tiles):
    R, D = h2.shape
    F = wg.shape[3]
    em = lambda t, te, tj0, tok, nact: (l, te[t], 0, 0)
    grid_spec = pltpu.PrefetchScalarGridSpec(
        num_scalar_prefetch=4,
        grid=(n_tiles,),
        in_specs=[pl.BlockSpec(memory_space=pl.ANY),
                  pl.BlockSpec((None, None, D, F), em), pl.BlockSpec((None, None, 1, F), em),
                  pl.BlockSpec((None, None, D, F), em), pl.BlockSpec((None, None, 1, F), em),
                  pl.BlockSpec((None, None, F, D), em), pl.BlockSpec((None, None, 1, D), em)],
        out_specs=pl.BlockSpec((tm, D), lambda t, te, tj0, tok, nact: (t, 0)),
        scratch_shapes=[pltpu.VMEM((2, tm, D), F32), pltpu.SemaphoreType.DMA((2,))],
    )
    return pl.pallas_call(
        functools.partial(_moe_kernel, tm=tm),
        grid_spec=grid_spec,
        out_shape=jax.ShapeDtypeStruct((n_tiles * tm, D), F32),
        compiler_params=_cp("arbitrary"),
        name="moe",
    )(te, tj0, tok, nact, h2, wg, bg, wu, bu, wd, bd)


def _dispatch(idx, R, E, tm, n_tiles):
    e_flat = idx.reshape(-1)
    n_pairs = e_flat.shape[0]
    oh = e_flat[:, None] == jnp.arange(E, dtype=jnp.int32)[None, :]
    cs = jnp.cumsum(oh.astype(jnp.int32), axis=0)
    rank = jnp.sum(jnp.where(oh, cs, 0), axis=1) - 1
    counts = cs[-1]
    tiles_e = (counts + tm - 1) // tm
    tile_end = jnp.cumsum(tiles_e)
    tile_start = tile_end - tiles_e
    nact = tile_end[-1]
    cum_start = jnp.cumsum(counts) - counts
    dest = jnp.sum(jnp.where(oh, tile_start[None, :], 0), axis=1) * tm + rank
    tok = jnp.arange(n_pairs, dtype=jnp.int32) // TOP_K
    sorted_tok = jnp.concatenate([jnp.sort(e_flat * R + tok) % R, jnp.zeros((tm,), jnp.int32)])
    tt = jnp.minimum(jnp.arange(n_tiles, dtype=jnp.int32), nact - 1)
    te = jnp.sum((tt[:, None] >= tile_end[None, :]).astype(jnp.int32), axis=1)
    te = jnp.minimum(te, E - 1)
    tj0 = cum_start[te] + (tt - tile_start[te]) * tm
    return (te.astype(jnp.int32), tj0.astype(jnp.int32), sorted_tok.astype(jnp.int32),
            nact.reshape(1).astype(jnp.int32), dest.astype(jnp.int32))


def _combine_kernel(dest_ref, y_hbm, wt_ref, x1_ref, g2x, g2c, sh1x, sh1c, sc1x, sc1c, lg_ref, lb_ref,
                    *rest, tc, nT, j_off, nC, alpha, last):
    if last:
        out_ref, buf, sem = rest
    else:
        xo_ref, h_ref, buf, sem = rest
    nJ = nT - j_off
    step = pl.program_id(0) * nJ + pl.program_id(1)
    n_steps = pl.num_programs(0) * nJ
    j = pl.program_id(1) + j_off

    def row_copy(p0, i, k, slot):
        return pltpu.make_async_copy(y_hbm.at[pl.ds(dest_ref[p0 + i * TOP_K + k], 1), :],
                                     buf.at[slot, k, pl.ds(i, 1), :], sem.at[slot])

    def pair0(s):
        return ((s // nJ) * nT + (s % nJ) + j_off) * (tc * TOP_K)

    def wait_step(slot):
        for k in range(TOP_K):
            pltpu.make_async_copy(y_hbm.at[pl.ds(0, tc), :], buf.at[slot, k], sem.at[slot]).wait()

    def gather(s, slot):
        p0 = pair0(s)

        def issue(i, carry):
            for k in range(TOP_K):
                row_copy(p0, i, k, slot).start()
            return carry

        lax.fori_loop(0, tc, issue, 0, unroll=4)

    @pl.when(step == 0)
    def _():
        gather(0, 0)

    @pl.when(step + 1 < n_steps)
    def _():
        gather(step + 1, (step + 1) % 2)

    slot = step % 2
    wait_step(slot)
    wt = wt_ref[...]
    f = wt[:, 0:1] * buf[slot, 0]
    for k in range(1, TOP_K):
        f = f + wt[:, k:k + 1] * buf[slot, k]
    is_ctx = j < nC
    g2 = jnp.where(is_ctx, g2c[...], g2x[...])
    x2 = _ln(alpha * x1_ref[...] + g2 * f) * lg_ref[...] + lb_ref[...]
    if last:
        out_ref[...] = x2
    else:
        sh1 = jnp.where(is_ctx, sh1c[...], sh1x[...])
        sc1 = jnp.where(is_ctx, sc1c[...], sc1x[...])
        xo_ref[...] = x2
        h_ref[...] = (_ln(x2) * (1.0 + sc1) + sh1).astype(BF16)


def _combine(ys, dest, wts, x1, mod3, mod3_next, ln_g, ln_b, B, S, C, alpha, last):
    R, D = x1.shape
    Tb = C + S
    tc = 128
    nC = C // tc
    nT = Tb // tc
    j_off = nC if last else 0
    nJ = nT - j_off
    bi = lambda b, j, d: b
    ci = lambda b, j, d: B
    row = pl.BlockSpec((tc, D), lambda b, j, d: (b * nT + j + j_off, 0))
    vec = pl.BlockSpec((1, D), lambda b, j, d: (0, 0))
    if last:
        out_specs = pl.BlockSpec((tc, D), lambda b, j, d: (b * nJ + j, 0))
        out_shape = jax.ShapeDtypeStruct((B * S, D), F32)
    else:
        out_specs = [row, row]
        out_shape = [jax.ShapeDtypeStruct((R, D), F32), jax.ShapeDtypeStruct((R, D), BF16)]
    grid_spec = pltpu.PrefetchScalarGridSpec(
        num_scalar_prefetch=1,
        grid=(B, nJ),
        in_specs=[pl.BlockSpec(memory_space=pl.ANY),
                  pl.BlockSpec((tc, LANES), lambda b, j, d: (b * nT + j + j_off, 0)),
                  row,
                  _mod_spec(D, bi, 5), _mod_spec(D, ci, 5), _mod_spec(D, bi, 0), _mod_spec(D, ci, 0),
                  _mod_spec(D, bi, 1), _mod_spec(D, ci, 1), vec, vec],
        out_specs=out_specs,
        scratch_shapes=[pltpu.VMEM((2, TOP_K, tc, D), F32), pltpu.SemaphoreType.DMA((2,))],
    )
    return pl.pallas_call(
        functools.partial(_combine_kernel, tc=tc, nT=nT, j_off=j_off, nC=nC, alpha=alpha, last=last),
        grid_spec=grid_spec,
        out_shape=out_shape,
        compiler_params=_cp("arbitrary", "arbitrary"),
        name="combine",
    )(dest, ys, wts, x1, mod3, mod3, mod3_next, mod3_next, mod3_next, mod3_next,
      ln_g.reshape(1, D), ln_b.reshape(1, D))


def _rope_tables(S, C, head_dim):
    rows = S // GRID_W
    pos_row = jnp.repeat(jnp.arange(rows, dtype=jnp.int32), GRID_W).astype(F32)
    pos_col = jnp.tile(jnp.arange(GRID_W, dtype=jnp.int32), rows).astype(F32)
    n = head_dim // 4
    inv = ROPE_BASE ** (-jnp.arange(n, dtype=F32) / n)
    ang = jnp.concatenate([pos_row[:, None] * inv, pos_col[:, None] * inv], -1)
    cos, sin = jnp.cos(ang), jnp.sin(ang)
    reps = LANES // head_dim
    cos_f = jnp.tile(jnp.concatenate([cos, cos], -1), (1, reps))
    sin_f = jnp.tile(jnp.concatenate([-sin, sin], -1), (1, reps))
    cos_f = jnp.concatenate([jnp.ones((C, LANES), F32), cos_f], 0)
    sin_f = jnp.concatenate([jnp.zeros((C, LANES), F32), sin_f], 0)
    return cos_f, sin_f


def kernel(x, c, ctx, c_ctx, w_ada, b_ada, w_in, conv_w, conv_b, lru_wr, lru_br, lru_wi, lru_bi, lru_lambda, wa_sink, da_lq1, da_lk1, da_lq2, da_lk2, da_norm_g, w_br_a, w_br_b, w_br_c, w_out, b_out, ln1_g, ln1_b, router_w, router_b, exp_w_gate, exp_b_gate, exp_w_up, exp_b_up, exp_w_down, exp_b_down, ln2_g, ln2_b):
    B, S, D = x.shape
    C = ctx.shape[1]
    depth = w_ada.shape[0]
    Tb = C + S
    R = B * Tb
    d_in = w_in.shape[2]
    d_rnn = w_br_a.shape[1]
    wa_w = w_br_b.shape[1]
    da_w = w_br_c.shape[1]
    kv_w = (d_in - d_rnn - wa_w - 3 * da_w - 3 * D) // 2
    wa_heads = wa_w // WA_HEAD_DIM
    kvh_n = kv_w // WA_HEAD_DIM
    G = wa_heads // kvh_n
    da_heads = da_w // (2 * DA_HEAD_DIM)
    nrb = lru_wr.shape[2]
    E = router_w.shape[2]
    alpha = (2 * depth) ** 0.25
    assert B + 1 <= 8 and lru_wr.shape[3] == LANES and S % GRID_W == 0
    q_off = d_rnn
    k_off = q_off + wa_w
    v_off = k_off + kv_w
    qd_off = v_off + kv_w
    kd_off = qd_off + da_w
    vd_off = kd_off + da_w
    g_off = vd_off + da_w

    cos_wa, sin_wa = _rope_tables(S, C, WA_HEAD_DIM)
    cos_da, sin_da = _rope_tables(S, C, DA_HEAD_DIM)

    cc = jnp.concatenate([c, c_ctx[None, :], jnp.zeros((8 - B - 1, D), F32)], axis=0)
    b_ada3 = b_ada.reshape(depth, 1, 6 * D)
    mods = [_ada(cc, w_ada, b_ada3, l).reshape(8, 1, 6 * D) for l in range(depth)]

    xall, h = _ln0(x.reshape(B * S, D), ctx.reshape(B * C, D), mods[0], B, S, C)

    w_bra_b = _cast_bf16(w_br_a, 256)
    w_brb_b = _cast_bf16(w_br_b, 256)
    w_brc_b = _cast_bf16(w_br_c, 256)
    wg_b = _cast_bf16(exp_w_gate, 2048)
    wu_b = _cast_bf16(exp_w_up, 2048)
    wd_b = _cast_bf16(exp_w_down, 256)
    F = exp_w_gate.shape[3]
    bg_e = exp_b_gate.reshape(depth, E, 1, F)
    bu_e = exp_b_up.reshape(depth, E, 1, F)
    bd_e = exp_b_down.reshape(depth, E, 1, D)
    conv_b3 = conv_b.reshape(depth, 1, d_rnn)

    tm_moe = 256
    n_pairs = R * TOP_K
    n_tiles = n_pairs // tm_moe + E

    out = None
    for l in range(depth):
        last = l == depth - 1
        lam_init = 0.8 - 0.6 * math.exp(-0.3 * l)
        proj = _mm(h, w_in, l, 1088, 512)
        wg = jnp.concatenate([lru_wr[l, 0], lru_wi[l, 0], lru_wr[l, 1], lru_wi[l, 1]], axis=-1).astype(BF16)
        bg = jnp.stack([lru_br[l, 0].reshape(nrb, LANES), lru_bi[l, 0].reshape(nrb, LANES),
                        lru_br[l, 1].reshape(nrb, LANES), lru_bi[l, 1].reshape(nrb, LANES)], axis=1)
        bg = bg.reshape(nrb, 1, 4 * LANES)
        lam2 = jnp.stack([lru_lambda[l, 0].reshape(nrb, LANES), lru_lambda[l, 1].reshape(nrb, LANES)], axis=1)
        lam2 = lam2.reshape(nrb, 1, 2 * LANES)
        ya = _rglru(proj, conv_w, conv_b3, wg, bg, lam2, l, B, S, C, d_rnn)
        yb = _wattn(proj, wa_sink[l], cos_wa, sin_wa, B, S, C, q_off, k_off, v_off, kvh_n, G)
        lqk = jnp.stack([da_lq1[l], da_lk1[l], da_lq2[l], da_lk2[l]], axis=0)
        yc = _dattn(proj, cos_da, sin_da, da_norm_g[l].reshape(1, LANES), lqk, B, S, C,
                    qd_off, kd_off, vd_off, da_heads, lam_init)
        m = _merge(ya, yb, yc, w_bra_b, w_brb_b, w_brc_b, l, proj, g_off, D)
        mo = _mm(m, w_out, l, 1088, 512)
        x1, h2, idx, wts = _post1(mo, xall, b_out[l], mods[l], ln1_g[l], ln1_b[l], router_w[l], router_b[l],
                                  B, S, C, alpha)
        te, tj0, tok, nact, dest = _dispatch(idx[:, :TOP_K], R, E, tm_moe, n_tiles)
        ys = _moe(h2, te, tj0, tok, nact, wg_b, bg_e, wu_b, bu_e, wd_b, bd_e, l, tm_moe, n_tiles)
        nxt = mods[l + 1] if not last else mods[l]
        res = _combine(ys, dest, wts, x1, mods[l], nxt, ln2_g[l], ln2_b[l], B, S, C, alpha, last)
        if last:
            out = res.reshape(B, S, D)
        else:
            xall, h = res
    return out
```

```python
import functools
import math

import jax
import jax.numpy as jnp
from jax import lax
from jax.experimental import pallas as pl
from jax.experimental.pallas import tpu as pltpu

F32 = jnp.float32
BF16 = jnp.bfloat16

LN_EPS = 1e-5
GRID_W = 64
ROPE_BASE = 10000.0
LRU_C = 8.0
WA_HEAD_DIM = 128
DA_HEAD_DIM = 64
WINDOW = 128
TOP_K = 4
SWIGLU_LIMIT = 7.0
SWIGLU_ALPHA = 1.702
LANES = 128
SUBLANES = 8
NEG = -1e30
LOG2E = 1.4426950408889634
VMEM_LIMIT = 56 * 1024 * 1024


def _cp(*sem):
    return pltpu.CompilerParams(dimension_semantics=sem, vmem_limit_bytes=VMEM_LIMIT)


def _sigmoid(v):
    return 1.0 / (1.0 + jnp.exp(-v))


def _ln(v):
    mu = jnp.mean(v, axis=-1, keepdims=True)
    d = v - mu
    var = jnp.mean(d * d, axis=-1, keepdims=True)
    return d * lax.rsqrt(var + LN_EPS)


def _ada_kernel(c_ref, w_ref, b_ref, o_ref):
    c = c_ref[...]
    s = (c * _sigmoid(c)).astype(BF16)
    o_ref[...] = jnp.dot(s, w_ref[...].astype(BF16), preferred_element_type=F32) + b_ref[...]


def _ada(cc, w, b, l):
    _, D, N = w.shape
    tn = 512
    return pl.pallas_call(
        _ada_kernel,
        grid=(N // tn,),
        in_specs=[pl.BlockSpec((8, D), lambda j: (0, 0)),
                  pl.BlockSpec((None, D, tn), lambda j: (l, 0, j)),
                  pl.BlockSpec((None, 1, tn), lambda j: (l, 0, j))],
        out_specs=pl.BlockSpec((8, tn), lambda j: (0, j)),
        out_shape=jax.ShapeDtypeStruct((8, N), F32),
        compiler_params=_cp("arbitrary"),
        name="ada",
    )(cc, w, b)


def _mod_spec(D, row_fn, chunk):
    return pl.BlockSpec((None, 1, D), lambda *a: (row_fn(*a), 0, chunk))


def _ln0_kernel(x_ref, c_ref, shx, scx, shc, scc, xo_ref, h_ref, *, nC):
    is_ctx = pl.program_id(1) < nC
    v = jnp.where(is_ctx, c_ref[...], x_ref[...])
    sh = jnp.where(is_ctx, shc[...], shx[...])
    sc = jnp.where(is_ctx, scc[...], scx[...])
    xo_ref[...] = v
    h_ref[...] = (_ln(v) * (1.0 + sc) + sh).astype(BF16)


def _ln0(x2, ctx2, mod3, B, S, C):
    D = x2.shape[1]
    Tb = C + S
    tr = 256
    nC = C // tr
    nS = S // tr
    nT = Tb // tr
    bi = lambda b, j: b
    ci = lambda b, j: B
    return pl.pallas_call(
        functools.partial(_ln0_kernel, nC=nC),
        grid=(B, nT),
        in_specs=[pl.BlockSpec((tr, D), lambda b, j: (b * nS + jnp.maximum(j - nC, 0), 0)),
                  pl.BlockSpec((tr, D), lambda b, j: (b * nC + jnp.minimum(j, nC - 1), 0)),
                  _mod_spec(D, bi, 0), _mod_spec(D, bi, 1), _mod_spec(D, ci, 0), _mod_spec(D, ci, 1)],
        out_specs=[pl.BlockSpec((tr, D), lambda b, j: (b * nT + j, 0)),
                   pl.BlockSpec((tr, D), lambda b, j: (b * nT + j, 0))],
        out_shape=[jax.ShapeDtypeStruct((B * Tb, D), F32), jax.ShapeDtypeStruct((B * Tb, D), BF16)],
        compiler_params=_cp("arbitrary", "arbitrary"),
        name="ln0",
    )(x2, ctx2, mod3, mod3, mod3, mod3)


def _mm_kernel(a_ref, w_ref, *rest, n_side):
    o_ref = rest[n_side]
    o_ref[...] = jnp.dot(a_ref[...], w_ref[...].astype(BF16), preferred_element_type=F32).astype(o_ref.dtype)
    for src, dst in zip(rest[:n_side], rest[n_side + 1:]):
        dst[...] = src[...].astype(BF16)


def _mm(a, w, l, tm, tn, side=()):
    M, K = a.shape
    N = w.shape[2]
    nj = N // tn
    in_specs = [pl.BlockSpec((tm, K), lambda i, j: (i, 0)),
                pl.BlockSpec((None, K, tn), lambda i, j: (l, 0, j))]
    out_specs = [pl.BlockSpec((tm, tn), lambda i, j: (i, j))]
    out_shape = [jax.ShapeDtypeStruct((M, N), BF16)]
    for arr, br in side:
        _, rows, cols = arr.shape
        last = rows // br - 1
        assert rows % br == 0 and last < (M // tm) * nj
        in_specs.append(pl.BlockSpec((None, br, cols), lambda i, j, last=last: (l, jnp.minimum(i * nj + j, last), 0)))
        out_specs.append(pl.BlockSpec((br, cols), lambda i, j, last=last: (jnp.minimum(i * nj + j, last), 0)))
        out_shape.append(jax.ShapeDtypeStruct((rows, cols), BF16))
    res = pl.pallas_call(
        functools.partial(_mm_kernel, n_side=len(side)),
        grid=(M // tm, nj),
        in_specs=in_specs,
        out_specs=out_specs,
        out_shape=out_shape,
        compiler_params=_cp("arbitrary", "arbitrary"),
        name="mm",
    )(a, w, *[arr for arr, _ in side])
    return res[0], res[1:]


def _rglru_kernel(u_ref, cw_ref, cb_ref, wg_ref, bg_ref, lam_ref, y_ref, h_scr, p_scr, *, C, S):
    T = C + S
    nv = T // SUBLANES
    u = u_ref[...].astype(F32)
    row = lax.broadcasted_iota(jnp.int32, (T, LANES), 0)
    pos = jnp.where(row < C, row, row - C)
    length = jnp.where(row < C, C, S)

    def tap(k):
        if k == 0:
            return u
        r = pltpu.roll(u, (-k) % T, 0)
        ok = (pos + k >= 0) & (pos + k < length)
        return jnp.where(ok, r, 0.0)

    cw = cw_ref[...]
    v = cb_ref[...] + cw[0:1] * tap(-1) + cw[1:2] * u + cw[2:3] * tap(1) + cw[3:4] * tap(2)
    g = jnp.dot(v.astype(BF16), wg_ref[...], preferred_element_type=F32) + bg_ref[...]
    lam = lam_ref[...]
    nl = -lam
    sp = jnp.maximum(nl, 0.0) + jnp.log1p(jnp.exp(-jnp.abs(nl)))

    sub = lax.broadcasted_iota(jnp.int32, (nv, SUBLANES, LANES), 1)
    for d in range(2):
        r = _sigmoid(g[:, (2 * d) * LANES:(2 * d + 1) * LANES])
        i = _sigmoid(g[:, (2 * d + 1) * LANES:(2 * d + 2) * LANES])
        a = jnp.exp((-LRU_C) * sp[:, d * LANES:(d + 1) * LANES] * r)
        b = jnp.sqrt((1.0 - a) * (1.0 + a)) * (i * v)
        a3 = a.reshape(nv, SUBLANES, LANES)
        b3 = b.reshape(nv, SUBLANES, LANES)
        for s in (1, 2, 4):
            if d == 0:
                ok = sub >= s
                shift = s
            else:
                ok = sub < SUBLANES - s
                shift = SUBLANES - s
            a_sh = jnp.where(ok, pltpu.roll(a3, shift, 1), 1.0)
            b_sh = jnp.where(ok, pltpu.roll(b3, shift, 1), 0.0)
            b3 = a3 * b_sh + b3
            a3 = a3 * a_sh
        h_scr[d] = b3.reshape(T, LANES)
        p_scr[d] = a3.reshape(T, LANES)

    nvc = C // SUBLANES

    def step(k, carry):
        cf, cr = carry
        rf = pl.multiple_of(k * SUBLANES, SUBLANES)
        of = h_scr[0, pl.ds(rf, SUBLANES), :] + p_scr[0, pl.ds(rf, SUBLANES), :] * cf
        h_scr[0, pl.ds(rf, SUBLANES), :] = of
        cf = jnp.broadcast_to(of[SUBLANES - 1:SUBLANES, :], (SUBLANES, LANES))
        kr = jnp.where(k < nvc, nvc - 1 - k, nv - 1 - (k - nvc))
        rr = pl.multiple_of(kr * SUBLANES, SUBLANES)
        orv = h_scr[1, pl.ds(rr, SUBLANES), :] + p_scr[1, pl.ds(rr, SUBLANES), :] * cr
        h_scr[1, pl.ds(rr, SUBLANES), :] = orv
        cr = jnp.broadcast_to(orv[0:1, :], (SUBLANES, LANES))
        return cf, cr

    z = jnp.zeros((SUBLANES, LANES), F32)
    lax.fori_loop(0, nv, step, (z, z))
    y_ref[...] = (h_scr[0] + h_scr[1]).astype(BF16)


def _rglru(proj, conv_w, conv_b, wg, bg, lam2, l, B, S, C, d_rnn):
    Tb = C + S
    nb = d_rnn // LANES
    return pl.pallas_call(
        functools.partial(_rglru_kernel, C=C, S=S),
        grid=(B, nb),
        in_specs=[pl.BlockSpec((Tb, LANES), lambda b, n: (b, n)),
                  pl.BlockSpec((None, 4, LANES), lambda b, n: (l, 0, n)),
                  pl.BlockSpec((None, 1, LANES), lambda b, n: (l, 0, n)),
                  pl.BlockSpec((None, LANES, 4 * LANES), lambda b, n: (n, 0, 0)),
                  pl.BlockSpec((None, 1, 4 * LANES), lambda b, n: (n, 0, 0)),
                  pl.BlockSpec((None, 1, 2 * LANES), lambda b, n: (n, 0, 0))],
        out_specs=pl.BlockSpec((Tb, LANES), lambda b, n: (b, n)),
        out_shape=jax.ShapeDtypeStruct((B * Tb, d_rnn), BF16),
        scratch_shapes=[pltpu.VMEM((2, Tb, LANES), F32), pltpu.VMEM((2, Tb, LANES), F32)],
        compiler_params=_cp("arbitrary", "arbitrary"),
        name="rglru",
    )(proj, conv_w, conv_b, wg, bg, lam2)


def _wattn_kernel(sink_ref, q_ref, k_ref, v_ref, cos_ref, sin_ref, o_ref, kr_scr, vt_scr, *, C, Tb, G):
    kvh = pl.program_id(1)
    j = pl.program_id(2)
    W = WINDOW
    nblk = Tb // W
    ncb = C // W
    scale = WA_HEAD_DIM ** -0.5
    nt = (((1,), (1,)), ((), ()))

    @pl.when(j == 0)
    def _():
        kf = k_ref[...].astype(F32)
        kr = kf * cos_ref[...] + pltpu.roll(kf, WA_HEAD_DIM // 2, 1) * sin_ref[...]
        zk = jnp.zeros((W, WA_HEAD_DIM), BF16)
        kr_scr[0:W, :] = zk
        kr_scr[W:W + Tb, :] = kr.astype(BF16)
        kr_scr[W + Tb:2 * W + Tb, :] = zk
        vt = v_ref[...].astype(F32).T.astype(BF16)
        vt_scr[0] = zk
        vt_scr[nblk + 1] = zk
        for bk in range(nblk):
            vt_scr[bk + 1] = vt[:, bk * W:(bk + 1) * W]

    r0 = pl.multiple_of(j * W, W)
    cs = cos_ref[pl.ds(r0, W), :]
    sn = sin_ref[pl.ds(r0, W), :]
    qs = []
    for g in range(G):
        qg = q_ref[:, g * WA_HEAD_DIM:(g + 1) * WA_HEAD_DIM].astype(F32)
        qs.append((qg * cs + pltpu.roll(qg, WA_HEAD_DIM // 2, 1) * sn) * scale)
    qq = jnp.concatenate(qs, axis=0).astype(BF16)
    sl = lax.dot_general(kr_scr[pl.ds(r0, 3 * W), :], qq, nt, preferred_element_type=F32)
    sx = lax.dot_general(kr_scr[W:W + C, :], qq, nt, preferred_element_type=F32)
    ki = lax.broadcasted_iota(jnp.int32, (3 * W, W), 0)
    qi = lax.broadcasted_iota(jnp.int32, (3 * W, W), 1)
    lo = C - (j - 1) * W
    hi = Tb - (j - 1) * W
    valid = (ki >= qi) & (ki <= qi + 2 * W) & (ki >= lo) & (ki < hi) & (j >= ncb)
    sl = jnp.concatenate([jnp.where(valid, sl[:, g * W:(g + 1) * W], NEG) for g in range(G)], axis=1)
    lane = lax.broadcasted_iota(jnp.int32, (1, G * W), 1)
    sink = jnp.zeros((1, G * W), F32)
    for g in range(G):
        sink = jnp.where(lane // W == g, sink_ref[kvh * G + g], sink)
    m = jnp.maximum(jnp.maximum(jnp.max(sl, axis=0, keepdims=True), jnp.max(sx, axis=0, keepdims=True)), sink)
    el = jnp.exp(sl - m)
    ex = jnp.exp(sx - m)
    l = jnp.sum(el, axis=0, keepdims=True) + jnp.sum(ex, axis=0, keepdims=True) + jnp.exp(sink - m)
    elb = el.astype(BF16)
    exb = ex.astype(BF16)
    ot = jnp.dot(vt_scr[j], elb[0:W], preferred_element_type=F32)
    ot = ot + jnp.dot(vt_scr[j + 1], elb[W:2 * W], preferred_element_type=F32)
    ot = ot + jnp.dot(vt_scr[j + 2], elb[2 * W:3 * W], preferred_element_type=F32)
    for cb in range(ncb):
        ot = ot + jnp.dot(vt_scr[1 + cb], exb[cb * W:(cb + 1) * W], preferred_element_type=F32)
    ot = ot * (1.0 / l)
    for g in range(G):
        o_ref[:, g * WA_HEAD_DIM:(g + 1) * WA_HEAD_DIM] = ot[:, g * W:(g + 1) * W].T.astype(BF16)


def _wattn(proj, sink, cos, sin, B, S, C, q_off, k_off, v_off, kvh_n, G):
    Tb = C + S
    nblk = Tb // WINDOW
    qw = G * WA_HEAD_DIM
    qc0 = q_off // qw
    kc0 = k_off // WA_HEAD_DIM
    vc0 = v_off // WA_HEAD_DIM
    assert C % WINDOW == 0 and WINDOW == WA_HEAD_DIM == LANES

    def slab(col0):
        return pl.BlockSpec((Tb, WA_HEAD_DIM), lambda b, h, j: (b, col0 + h))

    full = pl.BlockSpec((Tb, WA_HEAD_DIM), lambda b, h, j: (0, 0))
    return pl.pallas_call(
        functools.partial(_wattn_kernel, C=C, Tb=Tb, G=G),
        grid=(B, kvh_n, nblk),
        in_specs=[pl.BlockSpec(memory_space=pltpu.SMEM),
                  pl.BlockSpec((WINDOW, qw), lambda b, h, j: (b * nblk + j, qc0 + h)),
                  slab(kc0), slab(vc0), full, full],
        out_specs=pl.BlockSpec((WINDOW, qw), lambda b, h, j: (b * nblk + j, h)),
        out_shape=jax.ShapeDtypeStruct((B * Tb, kvh_n * qw), BF16),
        scratch_shapes=[pltpu.VMEM((Tb + 2 * WINDOW, WA_HEAD_DIM), BF16),
                        pltpu.VMEM((nblk + 2, WA_HEAD_DIM, WINDOW), BF16)],
        compiler_params=_cp("arbitrary", "arbitrary", "arbitrary"),
        name="wattn",
    )(sink, proj, proj, proj, cos, sin)


def _dattn_kernel(q_ref, k_ref, v_ref, cos_ref, sin_ref, g_ref, lqk_ref, o_ref, kr_scr, vt_scr, s0_scr, s1_scr,
                  *, C, S, tq, lam_init):
    Tb = C + S
    half = DA_HEAD_DIM // 2
    lane = lax.broadcasted_iota(jnp.int32, (1, LANES), 1)
    lo = (lane % DA_HEAD_DIM) < half
    nt = (((1,), (1,)), ((), ()))

    def rope(xv, cs, sn):
        rot = jnp.where(lo, pltpu.roll(xv, LANES - half, 1), pltpu.roll(xv, half, 1))
        return xv * cs + rot * sn

    kr_scr[...] = rope(k_ref[...].astype(F32), cos_ref[...], sin_ref[...]).astype(BF16)
    vt_scr[...] = v_ref[...].astype(F32).T.astype(BF16)
    lq = lqk_ref[...]
    lam = (jnp.exp(jnp.sum(lq[0:1] * lq[1:2], axis=1, keepdims=True))
           - jnp.exp(jnp.sum(lq[2:3] * lq[3:4], axis=1, keepdims=True)) + lam_init)
    gain = g_ref[...] * (1.0 - lam_init)

    def scores(r0, s_scr, nk):
        qv = rope(q_ref[pl.ds(r0, tq), :].astype(F32), cos_ref[pl.ds(r0, tq), :], sin_ref[pl.ds(r0, tq), :]) * (
            DA_HEAD_DIM ** -0.5 * LOG2E)
        qq = jnp.concatenate([jnp.where(lane < DA_HEAD_DIM, qv, 0.0), jnp.where(lane >= DA_HEAD_DIM, qv, 0.0)],
                             axis=0).astype(BF16)
        s = lax.dot_general(kr_scr[0:nk, :], qq, nt, preferred_element_type=F32)
        s_scr[0:nk, :] = s
        return jnp.max(s, axis=0, keepdims=True)

    def finish(r0, s_scr, m, nk):
        e = jnp.exp2(s_scr[0:nk, :] - m)
        l = jnp.sum(e, axis=0, keepdims=True)
        ot = jnp.dot(vt_scr[:, 0:nk], e.astype(BF16), preferred_element_type=F32)
        o = (ot[:, 0:tq] * (1.0 / l[:, 0:tq]) - ot[:, tq:2 * tq] * (lam / l[:, tq:2 * tq])).T
        o = o * lax.rsqrt(jnp.mean(o * o, axis=1, keepdims=True) + LN_EPS) * gain
        o_ref[pl.ds(r0, tq), :] = o.astype(BF16)

    for i in range(C // tq):
        finish(i * tq, s0_scr, scores(i * tq, s0_scr, C), C)

    n_lat = S // tq
    m0 = scores(C, s0_scr, Tb)

    def pair(p, m0):
        r = pl.multiple_of(C + p * (2 * tq), tq)
        m1 = scores(r + tq, s1_scr, Tb)
        finish(r, s0_scr, m0, Tb)
        m0 = scores(r + 2 * tq, s0_scr, Tb)
        finish(r + tq, s1_scr, m1, Tb)
        return m0

    m0 = lax.fori_loop(0, n_lat // 2 - 1, pair, m0)
    r = C + (n_lat - 2) * tq
    m1 = scores(r + tq, s1_scr, Tb)
    finish(r, s0_scr, m0, Tb)
    finish(r + tq, s1_scr, m1, Tb)


def _dattn(proj, cos, sin, g, lqk, B, S, C, q_off, k_off, v_off, heads, lam_init):
    Tb = C + S
    tq = 256
    assert C % tq == 0 and (S // tq) % 2 == 0 and S // tq >= 2
    qc0, kc0, vc0 = q_off // LANES, k_off // LANES, v_off // LANES
    full = pl.BlockSpec((Tb, LANES), lambda b, h: (0, 0))

    def slab(col0):
        return pl.BlockSpec((Tb, LANES), lambda b, h: (b, col0 + h))

    return pl.pallas_call(
        functools.partial(_dattn_kernel, C=C, S=S, tq=tq, lam_init=lam_init),
        grid=(B, heads),
        in_specs=[slab(qc0), slab(kc0), slab(vc0), full, full,
                  pl.BlockSpec((1, LANES), lambda b, h: (0, 0)),
                  pl.BlockSpec((4, DA_HEAD_DIM), lambda b, h: (0, 0))],
        out_specs=pl.BlockSpec((Tb, LANES), lambda b, h: (b, h)),
        out_shape=jax.ShapeDtypeStruct((B * Tb, heads * LANES), BF16),
        scratch_shapes=[pltpu.VMEM((Tb, LANES), BF16), pltpu.VMEM((LANES, Tb), BF16),
                        pltpu.VMEM((Tb, 2 * tq), F32), pltpu.VMEM((Tb, 2 * tq), F32)],
        compiler_params=_cp("arbitrary", "arbitrary"),
        name="dattn",
    )(proj, proj, proj, cos, sin, g, lqk)


def _merge_kernel(ya_ref, yb_ref, yc_ref, wa_ref, wb_ref, wc_ref, ga_ref, gb_ref, gc_ref, o_ref):
    def br(y_ref, w_ref, g_ref):
        return _sigmoid(g_ref[...].astype(F32)) * jnp.dot(y_ref[...], w_ref[...], preferred_element_type=F32)
    o_ref[...] = (br(ya_ref, wa_ref, ga_ref) + br(yb_ref, wb_ref, gb_ref) + br(yc_ref, wc_ref, gc_ref)).astype(BF16)


def _merge(ya, yb, yc, wa, wb, wc, proj, g_off, D):
    R = ya.shape[0]
    tm, tn = 544, 512
    g0 = g_off // tn
    nd = D // tn

    def yspec(y):
        return pl.BlockSpec((tm, y.shape[1]), lambda i, j: (i, 0))

    def wspec(w):
        return pl.BlockSpec((w.shape[0], tn), lambda i, j: (0, j))

    def gspec(k):
        return pl.BlockSpec((tm, tn), lambda i, j: (i, g0 + k * nd + j))

    return pl.pallas_call(
        _merge_kernel,
        grid=(R // tm, nd),
        in_specs=[yspec(ya), yspec(yb), yspec(yc), wspec(wa), wspec(wb), wspec(wc), gspec(0), gspec(1), gspec(2)],
        out_specs=pl.BlockSpec((tm, tn), lambda i, j: (i, j)),
        out_shape=jax.ShapeDtypeStruct((R, D), BF16),
        compiler_params=_cp("arbitrary", "arbitrary"),
        name="merge",
    )(ya, yb, yc, wa, wb, wc, proj, proj, proj)


def _post1_kernel(mo_ref, x_ref, bo_ref, g1x, g1c, sh2x, sh2c, sc2x, sc2c, lg_ref, lb_ref, rw_ref, rb_ref,
                  x1_ref, h2_ref, idx_ref, wt_ref, *, nC, alpha, E):
    is_ctx = pl.program_id(1) < nC
    g1 = jnp.where(is_ctx, g1c[...], g1x[...])
    sh2 = jnp.where(is_ctx, sh2c[...], sh2x[...])
    sc2 = jnp.where(is_ctx, sc2c[...], sc2x[...])
    y = alpha * x_ref[...] + g1 * (mo_ref[...].astype(F32) + bo_ref[...])
    x1 = _ln(y) * lg_ref[...] + lb_ref[...]
    x1_ref[...] = x1
    h2 = _ln(x1) * (1.0 + sc2) + sh2
    h2_ref[...] = h2
    logits = jnp.dot(h2, rw_ref[...], preferred_element_type=F32, precision=lax.Precision.HIGHEST) + rb_ref[...]
    col = lax.broadcasted_iota(jnp.int32, logits.shape, 1)
    lane = lax.broadcasted_iota(jnp.int32, (logits.shape[0], LANES), 1)
    idx_out = jnp.zeros((logits.shape[0], LANES), jnp.int32)
    val_out = jnp.zeros((logits.shape[0], LANES), F32)
    lg = logits
    v0 = None
    den = 0.0
    for k in range(TOP_K):
        m = jnp.max(lg, axis=1, keepdims=True)
        am = jnp.min(jnp.where(lg == m, col, E), axis=1, keepdims=True)
        lg = jnp.where(col == am, -jnp.inf, lg)
        if k == 0:
            v0 = m
        e = jnp.exp(m - v0)
        den = den + e
        idx_out = jnp.where(lane == k, am, idx_out)
        val_out = jnp.where(lane == k, e, val_out)
    idx_ref[...] = idx_out
    wt_ref[...] = val_out / den


def _post1(mo, xall, b_out, mod3, ln_g, ln_b, rw, rb, B, S, C, alpha):
    R, D = xall.shape
    E = rw.shape[1]
    Tb = C + S
    tr = 256
    nC = C // tr
    nT = Tb // tr
    bi = lambda b, j: b
    ci = lambda b, j: B
    row = pl.BlockSpec((tr, D), lambda b, j: (b * nT + j, 0))
    vec = pl.BlockSpec((1, D), lambda b, j: (0, 0))
    small = pl.BlockSpec((tr, LANES), lambda b, j: (b * nT + j, 0))
    return pl.pallas_call(
        functools.partial(_post1_kernel, nC=nC, alpha=alpha, E=E),
        grid=(B, nT),
        in_specs=[row, row, vec,
                  _mod_spec(D, bi, 2), _mod_spec(D, ci, 2), _mod_spec(D, bi, 3), _mod_spec(D, ci, 3),
                  _mod_spec(D, bi, 4), _mod_spec(D, ci, 4), vec, vec,
                  pl.BlockSpec((D, E), lambda b, j: (0, 0)), pl.BlockSpec((1, E), lambda b, j: (0, 0))],
        out_specs=[row, row, small, small],
        out_shape=[jax.ShapeDtypeStruct((R, D), F32), jax.ShapeDtypeStruct((R, D), F32),
                   jax.ShapeDtypeStruct((R, LANES), jnp.int32), jax.ShapeDtypeStruct((R, LANES), F32)],
        compiler_params=_cp("arbitrary", "arbitrary"),
        name="post1",
    )(mo, xall, b_out.reshape(1, D), mod3, mod3, mod3, mod3, mod3, mod3,
      ln_g.reshape(1, D), ln_b.reshape(1, D), rw, rb.reshape(1, E))


def _moe_kernel(te_ref, tj0_ref, tok_ref, nact_ref, h_hbm, wg_ref, bg_ref, wu_ref, bu_ref, wd_ref, bd_ref,
                y_ref, buf, sem, *, tm):
    t = pl.program_id(0)
    nact = nact_ref[0]

    def row_copy(j0, i, slot):
        return pltpu.make_async_copy(h_hbm.at[pl.ds(tok_ref[j0 + i], 1), :], buf.at[slot, pl.ds(i, 1), :],
                                     sem.at[slot])

    def wait_tile(slot):
        pltpu.make_async_copy(h_hbm.at[pl.ds(0, tm), :], buf.at[slot], sem.at[slot]).wait()

    def gather(tile, slot):
        j0 = tj0_ref[tile]

        def issue(i, carry):
            row_copy(j0, i, slot).start()
            return carry

        lax.fori_loop(0, tm, issue, 0, unroll=8)

    @pl.when(t == 0)
    def _():
        gather(0, 0)

    @pl.when(t + 1 < nact)
    def _():
        gather(t + 1, (t + 1) % 2)

    @pl.when(t < nact)
    def _():
        slot = t % 2
        wait_tile(slot)
        xb = buf[slot].astype(BF16)
        gate = jnp.minimum(jnp.dot(xb, wg_ref[...], preferred_element_type=F32) + bg_ref[...], SWIGLU_LIMIT)
        up = jnp.clip(jnp.dot(xb, wu_ref[...], preferred_element_type=F32) + bu_ref[...], -SWIGLU_LIMIT, SWIGLU_LIMIT)
        act = ((up + 1.0) * gate * _sigmoid(SWIGLU_ALPHA * gate)).astype(BF16)
        y_ref[...] = jnp.dot(act, wd_ref[...], preferred_element_type=F32) + bd_ref[...]

    @pl.when(t >= nact)
    def _():
        y_ref[...] = jnp.zeros_like(y_ref)


def _moe(h2, te, tj0, tok, nact, wg, bg, wu, bu, wd, bd, l, tm, n_tiles):
    R, D = h2.shape
    F = wg.shape[2]
    em = lambda t, te, tj0, tok, nact: (te[t], 0, 0)
    eb = lambda t, te, tj0, tok, nact: (l, te[t], 0, 0)
    grid_spec = pltpu.PrefetchScalarGridSpec(
        num_scalar_prefetch=4,
        grid=(n_tiles,),
        in_specs=[pl.BlockSpec(memory_space=pl.ANY),
                  pl.BlockSpec((None, D, F), em), pl.BlockSpec((None, None, 1, F), eb),
                  pl.BlockSpec((None, D, F), em), pl.BlockSpec((None, None, 1, F), eb),
                  pl.BlockSpec((None, F, D), em), pl.BlockSpec((None, None, 1, D), eb)],
        out_specs=pl.BlockSpec((tm, D), lambda t, te, tj0, tok, nact: (t, 0)),
        scratch_shapes=[pltpu.VMEM((2, tm, D), F32), pltpu.SemaphoreType.DMA((2,))],
    )
    return pl.pallas_call(
        functools.partial(_moe_kernel, tm=tm),
        grid_spec=grid_spec,
        out_shape=jax.ShapeDtypeStruct((n_tiles * tm, D), F32),
        compiler_params=_cp("arbitrary"),
        name="moe",
    )(te, tj0, tok, nact, h2, wg, bg, wu, bu, wd, bd)


def _dispatch(idx, R, E, tm, n_tiles):
    e_flat = idx.reshape(-1)
    n_pairs = e_flat.shape[0]
    oh = e_flat[:, None] == jnp.arange(E, dtype=jnp.int32)[None, :]
    cs = jnp.cumsum(oh.astype(jnp.int32), axis=0)
    rank = jnp.sum(jnp.where(oh, cs, 0), axis=1) - 1
    counts = cs[-1]
    tiles_e = (counts + tm - 1) // tm
    tile_end = jnp.cumsum(tiles_e)
    tile_start = tile_end - tiles_e
    nact = tile_end[-1]
    cum_start = jnp.cumsum(counts) - counts
    dest = jnp.sum(jnp.where(oh, tile_start[None, :], 0), axis=1) * tm + rank
    tok = jnp.arange(n_pairs, dtype=jnp.int32) // TOP_K
    sorted_tok = jnp.concatenate([jnp.sort(e_flat * R + tok) % R, jnp.zeros((tm,), jnp.int32)])
    tt = jnp.minimum(jnp.arange(n_tiles, dtype=jnp.int32), nact - 1)
    te = jnp.sum((tt[:, None] >= tile_end[None, :]).astype(jnp.int32), axis=1)
    te = jnp.minimum(te, E - 1)
    tj0 = cum_start[te] + (tt - tile_start[te]) * tm
    return (te.astype(jnp.int32), tj0.astype(jnp.int32), sorted_tok.astype(jnp.int32),
            nact.reshape(1).astype(jnp.int32), dest.astype(jnp.int32))


def _combine_kernel(dest_ref, y_hbm, wt_ref, x1_ref, g2x, g2c, sh1x, sh1c, sc1x, sc1c, lg_ref, lb_ref,
                    *rest, tc, nT, j_off, nC, alpha, last):
    if last:
        out_ref, buf, sem = rest
    else:
        xo_ref, h_ref, buf, sem = rest
    nJ = nT - j_off
    step = pl.program_id(0) * nJ + pl.program_id(1)
    n_steps = pl.num_programs(0) * nJ
    j = pl.program_id(1) + j_off

    def row_copy(p0, i, k, slot):
        return pltpu.make_async_copy(y_hbm.at[pl.ds(dest_ref[p0 + i * TOP_K + k], 1), :],
                                     buf.at[slot, k, pl.ds(i, 1), :], sem.at[slot])

    def pair0(s):
        return ((s // nJ) * nT + (s % nJ) + j_off) * (tc * TOP_K)

    def wait_step(slot):
        for k in range(TOP_K):
            pltpu.make_async_copy(y_hbm.at[pl.ds(0, tc), :], buf.at[slot, k], sem.at[slot]).wait()

    def gather(s, slot):
        p0 = pair0(s)

        def issue(i, carry):
            for k in range(TOP_K):
                row_copy(p0, i, k, slot).start()
            return carry

        lax.fori_loop(0, tc, issue, 0, unroll=4)

    @pl.when(step == 0)
    def _():
        gather(0, 0)

    @pl.when(step + 1 < n_steps)
    def _():
        gather(step + 1, (step + 1) % 2)

    slot = step % 2
    wait_step(slot)
    wt = wt_ref[...]
    f = wt[:, 0:1] * buf[slot, 0]
    for k in range(1, TOP_K):
        f = f + wt[:, k:k + 1] * buf[slot, k]
    is_ctx = j < nC
    g2 = jnp.where(is_ctx, g2c[...], g2x[...])
    x2 = _ln(alpha * x1_ref[...] + g2 * f) * lg_ref[...] + lb_ref[...]
    if last:
        out_ref[...] = x2
    else:
        sh1 = jnp.where(is_ctx, sh1c[...], sh1x[...])
        sc1 = jnp.where(is_ctx, sc1c[...], sc1x[...])
        xo_ref[...] = x2
        h_ref[...] = (_ln(x2) * (1.0 + sc1) + sh1).astype(BF16)


def _combine(ys, dest, wts, x1, mod3, mod3_next, ln_g, ln_b, B, S, C, alpha, last):
    R, D = x1.shape
    Tb = C + S
    tc = 128
    nC = C // tc
    nT = Tb // tc
    j_off = nC if last else 0
    nJ = nT - j_off
    bi = lambda b, j, d: b
    ci = lambda b, j, d: B
    row = pl.BlockSpec((tc, D), lambda b, j, d: (b * nT + j + j_off, 0))
    vec = pl.BlockSpec((1, D), lambda b, j, d: (0, 0))
    if last:
        out_specs = pl.BlockSpec((tc, D), lambda b, j, d: (b * nJ + j, 0))
        out_shape = jax.ShapeDtypeStruct((B * S, D), F32)
    else:
        out_specs = [row, row]
        out_shape = [jax.ShapeDtypeStruct((R, D), F32), jax.ShapeDtypeStruct((R, D), BF16)]
    grid_spec = pltpu.PrefetchScalarGridSpec(
        num_scalar_prefetch=1,
        grid=(B, nJ),
        in_specs=[pl.BlockSpec(memory_space=pl.ANY),
                  pl.BlockSpec((tc, LANES), lambda b, j, d: (b * nT + j + j_off, 0)),
                  row,
                  _mod_spec(D, bi, 5), _mod_spec(D, ci, 5), _mod_spec(D, bi, 0), _mod_spec(D, ci, 0),
                  _mod_spec(D, bi, 1), _mod_spec(D, ci, 1), vec, vec],
        out_specs=out_specs,
        scratch_shapes=[pltpu.VMEM((2, TOP_K, tc, D), F32), pltpu.SemaphoreType.DMA((2,))],
    )
    return pl.pallas_call(
        functools.partial(_combine_kernel, tc=tc, nT=nT, j_off=j_off, nC=nC, alpha=alpha, last=last),
        grid_spec=grid_spec,
        out_shape=out_shape,
        compiler_params=_cp("arbitrary", "arbitrary"),
        name="combine",
    )(dest, ys, wts, x1, mod3, mod3, mod3_next, mod3_next, mod3_next, mod3_next,
      ln_g.reshape(1, D), ln_b.reshape(1, D))


def _rope_tables(S, C, head_dim):
    rows = S // GRID_W
    pos_row = jnp.repeat(jnp.arange(rows, dtype=jnp.int32), GRID_W).astype(F32)
    pos_col = jnp.tile(jnp.arange(GRID_W, dtype=jnp.int32), rows).astype(F32)
    n = head_dim // 4
    inv = ROPE_BASE ** (-jnp.arange(n, dtype=F32) / n)
    ang = jnp.concatenate([pos_row[:, None] * inv, pos_col[:, None] * inv], -1)
    cos, sin = jnp.cos(ang), jnp.sin(ang)
    reps = LANES // head_dim
    cos_f = jnp.tile(jnp.concatenate([cos, cos], -1), (1, reps))
    sin_f = jnp.tile(jnp.concatenate([-sin, sin], -1), (1, reps))
    cos_f = jnp.concatenate([jnp.ones((C, LANES), F32), cos_f], 0)
    sin_f = jnp.concatenate([jnp.zeros((C, LANES), F32), sin_f], 0)
    return cos_f, sin_f


def kernel(x, c, ctx, c_ctx, w_ada, b_ada, w_in, conv_w, conv_b, lru_wr, lru_br, lru_wi, lru_bi, lru_lambda, wa_sink, da_lq1, da_lk1, da_lq2, da_lk2, da_norm_g, w_br_a, w_br_b, w_br_c, w_out, b_out, ln1_g, ln1_b, router_w, router_b, exp_w_gate, exp_b_gate, exp_w_up, exp_b_up, exp_w_down, exp_b_down, ln2_g, ln2_b):
    B, S, D = x.shape
    C = ctx.shape[1]
    depth = w_ada.shape[0]
    Tb = C + S
    R = B * Tb
    d_in = w_in.shape[2]
    d_rnn = w_br_a.shape[1]
    wa_w = w_br_b.shape[1]
    da_w = w_br_c.shape[1]
    kv_w = (d_in - d_rnn - wa_w - 3 * da_w - 3 * D) // 2
    wa_heads = wa_w // WA_HEAD_DIM
    kvh_n = kv_w // WA_HEAD_DIM
    G = wa_heads // kvh_n
    da_heads = da_w // (2 * DA_HEAD_DIM)
    nrb = lru_wr.shape[2]
    E = router_w.shape[2]
    alpha = (2 * depth) ** 0.25
    assert B + 1 <= 8 and lru_wr.shape[3] == LANES and S % GRID_W == 0
    q_off = d_rnn
    k_off = q_off + wa_w
    v_off = k_off + kv_w
    qd_off = v_off + kv_w
    kd_off = qd_off + da_w
    vd_off = kd_off + da_w
    g_off = vd_off + da_w

    cos_wa, sin_wa = _rope_tables(S, C, WA_HEAD_DIM)
    cos_da, sin_da = _rope_tables(S, C, DA_HEAD_DIM)

    cc = jnp.concatenate([c, c_ctx[None, :], jnp.zeros((8 - B - 1, D), F32)], axis=0)
    b_ada3 = b_ada.reshape(depth, 1, 6 * D)
    mods = [_ada(cc, w_ada, b_ada3, l).reshape(8, 1, 6 * D) for l in range(depth)]

    xall, h = _ln0(x.reshape(B * S, D), ctx.reshape(B * C, D), mods[0], B, S, C)

    F = exp_w_gate.shape[3]
    wg3 = exp_w_gate.reshape(depth, E * D, F)
    wu3 = exp_w_up.reshape(depth, E * D, F)
    wd3 = exp_w_down.reshape(depth, E * F, D)
    bg_e = exp_b_gate.reshape(depth, E, 1, F)
    bu_e = exp_b_up.reshape(depth, E, 1, F)
    bd_e = exp_b_down.reshape(depth, E, 1, D)
    conv_b3 = conv_b.reshape(depth, 1, d_rnn)

    tm_moe = 256
    n_pairs = R * TOP_K
    n_tiles = n_pairs // tm_moe + E

    out = None
    for l in range(depth):
        last = l == depth - 1
        lam_init = 0.8 - 0.6 * math.exp(-0.3 * l)
        proj, (wg_b, wu_b, wd_b, wa_b, wb_b, wc_b) = _mm(
            h, w_in, l, 1088, 512,
            side=((wg3, E * D // 256), (wu3, E * D // 256), (wd3, E * F // 256),
                  (w_br_a, 16), (w_br_b, 16), (w_br_c, 16)))
        wg = jnp.concatenate([lru_wr[l, 0], lru_wi[l, 0], lru_wr[l, 1], lru_wi[l, 1]], axis=-1).astype(BF16)
        bg = jnp.stack([lru_br[l, 0].reshape(nrb, LANES), lru_bi[l, 0].reshape(nrb, LANES),
                        lru_br[l, 1].reshape(nrb, LANES), lru_bi[l, 1].reshape(nrb, LANES)], axis=1)
        bg = bg.reshape(nrb, 1, 4 * LANES)
        lam2 = jnp.stack([lru_lambda[l, 0].reshape(nrb, LANES), lru_lambda[l, 1].reshape(nrb, LANES)], axis=1)
        lam2 = lam2.reshape(nrb, 1, 2 * LANES)
        ya = _rglru(proj, conv_w, conv_b3, wg, bg, lam2, l, B, S, C, d_rnn)
        yb = _wattn(proj, wa_sink[l], cos_wa, sin_wa, B, S, C, q_off, k_off, v_off, kvh_n, G)
        lqk = jnp.stack([da_lq1[l], da_lk1[l], da_lq2[l], da_lk2[l]], axis=0)
        yc = _dattn(proj, cos_da, sin_da, da_norm_g[l].reshape(1, LANES), lqk, B, S, C,
                    qd_off, kd_off, vd_off, da_heads, lam_init)
        m = _merge(ya, yb, yc, wa_b, wb_b, wc_b, proj, g_off, D)
        mo, _ = _mm(m, w_out, l, 1088, 512)
        x1, h2, idx, wts = _post1(mo, xall, b_out[l], mods[l], ln1_g[l], ln1_b[l], router_w[l], router_b[l],
                                  B, S, C, alpha)
        te, tj0, tok, nact, dest = _dispatch(idx[:, :TOP_K], R, E, tm_moe, n_tiles)
        ys = _moe(h2, te, tj0, tok, nact, wg_b.reshape(E, D, F), bg_e, wu_b.reshape(E, D, F), bu_e,
                  wd_b.reshape(E, F, D), bd_e, l, tm_moe, n_tiles)
        nxt = mods[l + 1] if not last else mods[l]
        res = _combine(ys, dest, wts, x1, mods[l], nxt, ln2_g[l], ln2_b[l], B, S, C, alpha, last)
        if last:
            out = res.reshape(B, S, D)
        else:
            xall, h = res
    return out
```

```python
import functools
import math

import jax
import jax.numpy as jnp
from jax import lax
from jax.experimental import pallas as pl
from jax.experimental.pallas import tpu as pltpu

F32 = jnp.float32
BF16 = jnp.bfloat16

LN_EPS = 1e-5
GRID_W = 64
ROPE_BASE = 10000.0
LRU_C = 8.0
WA_HEAD_DIM = 128
DA_HEAD_DIM = 64
WINDOW = 128
TOP_K = 4
SWIGLU_LIMIT = 7.0
SWIGLU_ALPHA = 1.702
LANES = 128
SUBLANES = 8
NEG = -1e30
LOG2E = 1.4426950408889634
DA_CHUNKS = 17
ONES_ROWS = 16
VMEM_LIMIT = 56 * 1024 * 1024


def _cp(*sem):
    return pltpu.CompilerParams(dimension_semantics=sem, vmem_limit_bytes=VMEM_LIMIT)


def _sigmoid(v):
    return 1.0 / (1.0 + jnp.exp(-v))


def _ln(v):
    mu = jnp.mean(v, axis=-1, keepdims=True)
    d = v - mu
    var = jnp.mean(d * d, axis=-1, keepdims=True)
    return d * lax.rsqrt(var + LN_EPS)


def _ada_kernel(c_ref, w_ref, b_ref, o_ref):
    c = c_ref[...]
    s = (c * _sigmoid(c)).astype(BF16)
    o_ref[...] = jnp.dot(s, w_ref[...].astype(BF16), preferred_element_type=F32) + b_ref[...]


def _ada(cc, w, b, l):
    _, D, N = w.shape
    tn = 512
    return pl.pallas_call(
        _ada_kernel,
        grid=(N // tn,),
        in_specs=[pl.BlockSpec((8, D), lambda j: (0, 0)),
                  pl.BlockSpec((None, D, tn), lambda j: (l, 0, j)),
                  pl.BlockSpec((None, 1, tn), lambda j: (l, 0, j))],
        out_specs=pl.BlockSpec((8, tn), lambda j: (0, j)),
        out_shape=jax.ShapeDtypeStruct((8, N), F32),
        compiler_params=_cp("arbitrary"),
        name="ada",
    )(cc, w, b)


def _mod_spec(D, row_fn, chunk):
    return pl.BlockSpec((None, 1, D), lambda *a: (row_fn(*a), 0, chunk))


def _ln0_kernel(x_ref, c_ref, shx, scx, shc, scc, xo_ref, h_ref, *, nC):
    is_ctx = pl.program_id(1) < nC
    v = jnp.where(is_ctx, c_ref[...], x_ref[...])
    sh = jnp.where(is_ctx, shc[...], shx[...])
    sc = jnp.where(is_ctx, scc[...], scx[...])
    xo_ref[...] = v
    h_ref[...] = (_ln(v) * (1.0 + sc) + sh).astype(BF16)


def _ln0(x2, ctx2, mod3, B, S, C):
    D = x2.shape[1]
    Tb = C + S
    tr = 256
    nC = C // tr
    nS = S // tr
    nT = Tb // tr
    bi = lambda b, j: b
    ci = lambda b, j: B
    return pl.pallas_call(
        functools.partial(_ln0_kernel, nC=nC),
        grid=(B, nT),
        in_specs=[pl.BlockSpec((tr, D), lambda b, j: (b * nS + jnp.maximum(j - nC, 0), 0)),
                  pl.BlockSpec((tr, D), lambda b, j: (b * nC + jnp.minimum(j, nC - 1), 0)),
                  _mod_spec(D, bi, 0), _mod_spec(D, bi, 1), _mod_spec(D, ci, 0), _mod_spec(D, ci, 1)],
        out_specs=[pl.BlockSpec((tr, D), lambda b, j: (b * nT + j, 0)),
                   pl.BlockSpec((tr, D), lambda b, j: (b * nT + j, 0))],
        out_shape=[jax.ShapeDtypeStruct((B * Tb, D), F32), jax.ShapeDtypeStruct((B * Tb, D), BF16)],
        compiler_params=_cp("arbitrary", "arbitrary"),
        name="ln0",
    )(x2, ctx2, mod3, mod3, mod3, mod3)


def _mm_kernel(a_ref, w_ref, *rest, n_side):
    o_ref = rest[n_side]
    o_ref[...] = jnp.dot(a_ref[...], w_ref[...].astype(BF16), preferred_element_type=F32).astype(o_ref.dtype)
    for src, dst in zip(rest[:n_side], rest[n_side + 1:]):
        dst[...] = src[...].astype(BF16)


def _mm(a, w, l, tm, tn, side=()):
    M, K = a.shape
    N = w.shape[2]
    nj = N // tn
    in_specs = [pl.BlockSpec((tm, K), lambda i, j: (i, 0)),
                pl.BlockSpec((None, K, tn), lambda i, j: (l, 0, j))]
    out_specs = [pl.BlockSpec((tm, tn), lambda i, j: (i, j))]
    out_shape = [jax.ShapeDtypeStruct((M, N), BF16)]
    for arr, br in side:
        _, rows, cols = arr.shape
        last = rows // br - 1
        assert rows % br == 0 and last < (M // tm) * nj
        in_specs.append(pl.BlockSpec((None, br, cols), lambda i, j, last=last: (l, jnp.minimum(i * nj + j, last), 0)))
        out_specs.append(pl.BlockSpec((br, cols), lambda i, j, last=last: (jnp.minimum(i * nj + j, last), 0)))
        out_shape.append(jax.ShapeDtypeStruct((rows, cols), BF16))
    res = pl.pallas_call(
        functools.partial(_mm_kernel, n_side=len(side)),
        grid=(M // tm, nj),
        in_specs=in_specs,
        out_specs=out_specs,
        out_shape=out_shape,
        compiler_params=_cp("arbitrary", "arbitrary"),
        name="mm",
    )(a, w, *[arr for arr, _ in side])
    return res[0], res[1:]


def _rglru_kernel(u_ref, cw_ref, cb_ref, wg_ref, bg_ref, lam_ref, y_ref, h_scr, p_scr, *, C, S):
    T = C + S
    nv = T // SUBLANES
    u = u_ref[...].astype(F32)
    row = lax.broadcasted_iota(jnp.int32, (T, LANES), 0)
    pos = jnp.where(row < C, row, row - C)
    length = jnp.where(row < C, C, S)

    def tap(k):
        if k == 0:
            return u
        r = pltpu.roll(u, (-k) % T, 0)
        ok = (pos + k >= 0) & (pos + k < length)
        return jnp.where(ok, r, 0.0)

    cw = cw_ref[...]
    v = cb_ref[...] + cw[0:1] * tap(-1) + cw[1:2] * u + cw[2:3] * tap(1) + cw[3:4] * tap(2)
    g = jnp.dot(v.astype(BF16), wg_ref[...], preferred_element_type=F32) + bg_ref[...]
    lam = lam_ref[...]
    nl = -lam
    sp = jnp.maximum(nl, 0.0) + jnp.log1p(jnp.exp(-jnp.abs(nl)))

    sub = lax.broadcasted_iota(jnp.int32, (nv, SUBLANES, LANES), 1)
    for d in range(2):
        r = _sigmoid(g[:, (2 * d) * LANES:(2 * d + 1) * LANES])
        i = _sigmoid(g[:, (2 * d + 1) * LANES:(2 * d + 2) * LANES])
        a = jnp.exp((-LRU_C) * sp[:, d * LANES:(d + 1) * LANES] * r)
        b = jnp.sqrt((1.0 - a) * (1.0 + a)) * (i * v)
        a3 = a.reshape(nv, SUBLANES, LANES)
        b3 = b.reshape(nv, SUBLANES, LANES)
        for s in (1, 2, 4):
            if d == 0:
                ok = sub >= s
                shift = s
            else:
                ok = sub < SUBLANES - s
                shift = SUBLANES - s
            a_sh = jnp.where(ok, pltpu.roll(a3, shift, 1), 1.0)
            b_sh = jnp.where(ok, pltpu.roll(b3, shift, 1), 0.0)
            b3 = a3 * b_sh + b3
            a3 = a3 * a_sh
        h_scr[d] = b3.reshape(T, LANES)
        p_scr[d] = a3.reshape(T, LANES)

    nvc = C // SUBLANES

    def step(k, carry):
        cf, cr = carry
        rf = pl.multiple_of(k * SUBLANES, SUBLANES)
        of = h_scr[0, pl.ds(rf, SUBLANES), :] + p_scr[0, pl.ds(rf, SUBLANES), :] * cf
        h_scr[0, pl.ds(rf, SUBLANES), :] = of
        cf = jnp.broadcast_to(of[SUBLANES - 1:SUBLANES, :], (SUBLANES, LANES))
        kr = jnp.where(k < nvc, nvc - 1 - k, nv - 1 - (k - nvc))
        rr = pl.multiple_of(kr * SUBLANES, SUBLANES)
        orv = h_scr[1, pl.ds(rr, SUBLANES), :] + p_scr[1, pl.ds(rr, SUBLANES), :] * cr
        h_scr[1, pl.ds(rr, SUBLANES), :] = orv
        cr = jnp.broadcast_to(orv[0:1, :], (SUBLANES, LANES))
        return cf, cr

    z = jnp.zeros((SUBLANES, LANES), F32)
    lax.fori_loop(0, nv, step, (z, z))
    y_ref[...] = (h_scr[0] + h_scr[1]).astype(BF16)


def _rglru(proj, conv_w, conv_b, wg, bg, lam2, l, B, S, C, d_rnn):
    Tb = C + S
    nb = d_rnn // LANES
    return pl.pallas_call(
        functools.partial(_rglru_kernel, C=C, S=S),
        grid=(B, nb),
        in_specs=[pl.BlockSpec((Tb, LANES), lambda b, n: (b, n)),
                  pl.BlockSpec((None, 4, LANES), lambda b, n: (l, 0, n)),
                  pl.BlockSpec((None, 1, LANES), lambda b, n: (l, 0, n)),
                  pl.BlockSpec((None, LANES, 4 * LANES), lambda b, n: (n, 0, 0)),
                  pl.BlockSpec((None, 1, 4 * LANES), lambda b, n: (n, 0, 0)),
                  pl.BlockSpec((None, 1, 2 * LANES), lambda b, n: (n, 0, 0))],
        out_specs=pl.BlockSpec((Tb, LANES), lambda b, n: (b, n)),
        out_shape=jax.ShapeDtypeStruct((B * Tb, d_rnn), BF16),
        scratch_shapes=[pltpu.VMEM((2, Tb, LANES), F32), pltpu.VMEM((2, Tb, LANES), F32)],
        compiler_params=_cp("arbitrary", "arbitrary"),
        name="rglru",
    )(proj, conv_w, conv_b, wg, bg, lam2)


def _wattn_kernel(sink_ref, q_ref, k_ref, v_ref, cos_ref, sin_ref, o_ref, kr_scr, vt_scr, *, C, Tb, G):
    kvh = pl.program_id(1)
    j = pl.program_id(2)
    W = WINDOW
    nblk = Tb // W
    ncb = C // W
    scale = WA_HEAD_DIM ** -0.5
    nt = (((1,), (1,)), ((), ()))

    @pl.when(j == 0)
    def _():
        kf = k_ref[...].astype(F32)
        kr = kf * cos_ref[...] + pltpu.roll(kf, WA_HEAD_DIM // 2, 1) * sin_ref[...]
        zk = jnp.zeros((W, WA_HEAD_DIM), BF16)
        kr_scr[0:W, :] = zk
        kr_scr[W:W + Tb, :] = kr.astype(BF16)
        kr_scr[W + Tb:2 * W + Tb, :] = zk
        vt = v_ref[...].astype(F32).T.astype(BF16)
        vt_scr[0] = zk
        vt_scr[nblk + 1] = zk
        for bk in range(nblk):
            vt_scr[bk + 1] = vt[:, bk * W:(bk + 1) * W]

    r0 = pl.multiple_of(j * W, W)
    cs = cos_ref[pl.ds(r0, W), :]
    sn = sin_ref[pl.ds(r0, W), :]
    qs = []
    for g in range(G):
        qg = q_ref[:, g * WA_HEAD_DIM:(g + 1) * WA_HEAD_DIM].astype(F32)
        qs.append((qg * cs + pltpu.roll(qg, WA_HEAD_DIM // 2, 1) * sn) * scale)
    qq = jnp.concatenate(qs, axis=0).astype(BF16)
    sl = lax.dot_general(kr_scr[pl.ds(r0, 3 * W), :], qq, nt, preferred_element_type=F32)
    sx = lax.dot_general(kr_scr[W:W + C, :], qq, nt, preferred_element_type=F32)
    ki = lax.broadcasted_iota(jnp.int32, (3 * W, W), 0)
    qi = lax.broadcasted_iota(jnp.int32, (3 * W, W), 1)
    lo = C - (j - 1) * W
    hi = Tb - (j - 1) * W
    valid = (ki >= qi) & (ki <= qi + 2 * W) & (ki >= lo) & (ki < hi) & (j >= ncb)
    sl = jnp.concatenate([jnp.where(valid, sl[:, g * W:(g + 1) * W], NEG) for g in range(G)], axis=1)
    lane = lax.broadcasted_iota(jnp.int32, (1, G * W), 1)
    sink = jnp.zeros((1, G * W), F32)
    for g in range(G):
        sink = jnp.where(lane // W == g, sink_ref[kvh * G + g], sink)
    m = jnp.maximum(jnp.maximum(jnp.max(sl, axis=0, keepdims=True), jnp.max(sx, axis=0, keepdims=True)), sink)
    el = jnp.exp(sl - m)
    ex = jnp.exp(sx - m)
    l = jnp.sum(el, axis=0, keepdims=True) + jnp.sum(ex, axis=0, keepdims=True) + jnp.exp(sink - m)
    elb = el.astype(BF16)
    exb = ex.astype(BF16)
    ot = jnp.dot(vt_scr[j], elb[0:W], preferred_element_type=F32)
    ot = ot + jnp.dot(vt_scr[j + 1], elb[W:2 * W], preferred_element_type=F32)
    ot = ot + jnp.dot(vt_scr[j + 2], elb[2 * W:3 * W], preferred_element_type=F32)
    for cb in range(ncb):
        ot = ot + jnp.dot(vt_scr[1 + cb], exb[cb * W:(cb + 1) * W], preferred_element_type=F32)
    ot = ot * (1.0 / l)
    for g in range(G):
        o_ref[:, g * WA_HEAD_DIM:(g + 1) * WA_HEAD_DIM] = ot[:, g * W:(g + 1) * W].T.astype(BF16)


def _wattn(proj, sink, cos, sin, B, S, C, q_off, k_off, v_off, kvh_n, G):
    Tb = C + S
    nblk = Tb // WINDOW
    qw = G * WA_HEAD_DIM
    qc0 = q_off // qw
    kc0 = k_off // WA_HEAD_DIM
    vc0 = v_off // WA_HEAD_DIM
    assert C % WINDOW == 0 and WINDOW == WA_HEAD_DIM == LANES

    def slab(col0):
        return pl.BlockSpec((Tb, WA_HEAD_DIM), lambda b, h, j: (b, col0 + h))

    full = pl.BlockSpec((Tb, WA_HEAD_DIM), lambda b, h, j: (0, 0))
    return pl.pallas_call(
        functools.partial(_wattn_kernel, C=C, Tb=Tb, G=G),
        grid=(B, kvh_n, nblk),
        in_specs=[pl.BlockSpec(memory_space=pltpu.SMEM),
                  pl.BlockSpec((WINDOW, qw), lambda b, h, j: (b * nblk + j, qc0 + h)),
                  slab(kc0), slab(vc0), full, full],
        out_specs=pl.BlockSpec((WINDOW, qw), lambda b, h, j: (b * nblk + j, h)),
        out_shape=jax.ShapeDtypeStruct((B * Tb, kvh_n * qw), BF16),
        scratch_shapes=[pltpu.VMEM((Tb + 2 * WINDOW, WA_HEAD_DIM), BF16),
                        pltpu.VMEM((nblk + 2, WA_HEAD_DIM, WINDOW), BF16)],
        compiler_params=_cp("arbitrary", "arbitrary", "arbitrary"),
        name="wattn",
    )(sink, proj, proj, proj, cos, sin)


def _dattn_kernel(q_ref, k_ref, v_ref, cos_ref, sin_ref, g_ref, lqk_ref, o_ref, kr_scr, vt_scr, s0_scr, s1_scr,
                  *, C, S, tq, lam_init):
    Tb = C + S
    half = DA_HEAD_DIM // 2
    lane = lax.broadcasted_iota(jnp.int32, (1, LANES), 1)
    lo = (lane % DA_HEAD_DIM) < half
    nt = (((1,), (1,)), ((), ()))

    def rope(xv, cs, sn):
        rot = jnp.where(lo, pltpu.roll(xv, LANES - half, 1), pltpu.roll(xv, half, 1))
        return xv * cs + rot * sn

    kr_scr[...] = rope(k_ref[...].astype(F32), cos_ref[...], sin_ref[...]).astype(BF16)
    vt_scr[0:LANES, :] = v_ref[...].astype(F32).T.astype(BF16)
    vt_scr[LANES:LANES + ONES_ROWS, :] = jnp.ones((ONES_ROWS, Tb), BF16)
    lq = lqk_ref[...]
    lam = (jnp.exp(jnp.sum(lq[0:1] * lq[1:2], axis=1, keepdims=True))
           - jnp.exp(jnp.sum(lq[2:3] * lq[3:4], axis=1, keepdims=True)) + lam_init)
    gain = g_ref[...] * (1.0 - lam_init)

    def make_qq(r0):
        qv = rope(q_ref[pl.ds(r0, tq), :].astype(F32), cos_ref[pl.ds(r0, tq), :], sin_ref[pl.ds(r0, tq), :]) * (
            DA_HEAD_DIM ** -0.5 * LOG2E)
        return jnp.concatenate([jnp.where(lane < DA_HEAD_DIM, qv, 0.0), jnp.where(lane >= DA_HEAD_DIM, qv, 0.0)],
                               axis=0).astype(BF16)

    def write_out(r0, otl):
        ot = otl[0:LANES]
        l = otl[LANES:LANES + 1]
        o = (ot[:, 0:tq] * (1.0 / l[:, 0:tq]) - ot[:, tq:2 * tq] * (lam / l[:, tq:2 * tq])).T
        o = o * lax.rsqrt(jnp.mean(o * o, axis=1, keepdims=True) + LN_EPS) * gain
        o_ref[pl.ds(r0, tq), :] = o.astype(BF16)

    def scores(r0, s_scr, nk):
        s = lax.dot_general(kr_scr[0:nk, :], make_qq(r0), nt, preferred_element_type=F32)
        s_scr[0:nk, :] = s
        return jnp.max(s, axis=0, keepdims=True)

    def finish(r0, s_scr, m, nk):
        e = jnp.exp2(s_scr[0:nk, :] - m)
        write_out(r0, jnp.dot(vt_scr[:, 0:nk], e.astype(BF16), preferred_element_type=F32))

    for i in range(C // tq):
        finish(i * tq, s0_scr, scores(i * tq, s0_scr, C), C)

    n_lat = S // tq
    m0 = scores(C, s0_scr, Tb)

    units = Tb // tq
    cuts = [round(units * c / DA_CHUNKS) * tq for c in range(DA_CHUNKS + 1)]

    def fused(r_s, s_w, r_f, s_r, m_f):
        qq = make_qq(r_s)
        m_s = ot = None
        for c in range(DA_CHUNKS):
            k0, k1 = cuts[c], cuts[c + 1]
            s = lax.dot_general(kr_scr[k0:k1, :], qq, nt, preferred_element_type=F32)
            s_w[k0:k1, :] = s
            cm = jnp.max(s, axis=0, keepdims=True)
            m_s = cm if c == 0 else jnp.maximum(m_s, cm)
            e = jnp.exp2(s_r[k0:k1, :] - m_f)
            co = jnp.dot(vt_scr[:, k0:k1], e.astype(BF16), preferred_element_type=F32)
            ot = co if c == 0 else ot + co
        write_out(r_f, ot)
        return m_s

    def pair(p, m0):
        r = pl.multiple_of(C + p * (2 * tq), tq)
        m1 = fused(r + tq, s1_scr, r, s0_scr, m0)
        return fused(r + 2 * tq, s0_scr, r + tq, s1_scr, m1)

    m0 = lax.fori_loop(0, n_lat // 2 - 1, pair, m0)
    r = C + (n_lat - 2) * tq
    m1 = fused(r + tq, s1_scr, r, s0_scr, m0)
    finish(r + tq, s1_scr, m1, Tb)


def _dattn(proj, cos, sin, g, lqk, B, S, C, q_off, k_off, v_off, heads, lam_init):
    Tb = C + S
    tq = 256
    assert C % tq == 0 and (S // tq) % 2 == 0 and S // tq >= 2
    qc0, kc0, vc0 = q_off // LANES, k_off // LANES, v_off // LANES
    full = pl.BlockSpec((Tb, LANES), lambda b, h: (0, 0))

    def slab(col0):
        return pl.BlockSpec((Tb, LANES), lambda b, h: (b, col0 + h))

    return pl.pallas_call(
        functools.partial(_dattn_kernel, C=C, S=S, tq=tq, lam_init=lam_init),
        grid=(B, heads),
        in_specs=[slab(qc0), slab(kc0), slab(vc0), full, full,
                  pl.BlockSpec((1, LANES), lambda b, h: (0, 0)),
                  pl.BlockSpec((4, DA_HEAD_DIM), lambda b, h: (0, 0))],
        out_specs=pl.BlockSpec((Tb, LANES), lambda b, h: (b, h)),
        out_shape=jax.ShapeDtypeStruct((B * Tb, heads * LANES), BF16),
        scratch_shapes=[pltpu.VMEM((Tb, LANES), BF16), pltpu.VMEM((LANES + ONES_ROWS, Tb), BF16),
                        pltpu.VMEM((Tb, 2 * tq), F32), pltpu.VMEM((Tb, 2 * tq), F32)],
        compiler_params=_cp("arbitrary", "arbitrary"),
        name="dattn",
    )(proj, proj, proj, cos, sin, g, lqk)


def _merge_kernel(ya_ref, yb_ref, yc_ref, wa_ref, wb_ref, wc_ref, ga_ref, gb_ref, gc_ref, o_ref):
    def br(y_ref, w_ref, g_ref):
        return _sigmoid(g_ref[...].astype(F32)) * jnp.dot(y_ref[...], w_ref[...], preferred_element_type=F32)
    o_ref[...] = (br(ya_ref, wa_ref, ga_ref) + br(yb_ref, wb_ref, gb_ref) + br(yc_ref, wc_ref, gc_ref)).astype(BF16)


def _merge(ya, yb, yc, wa, wb, wc, proj, g_off, D):
    R = ya.shape[0]
    tm, tn = 544, 512
    g0 = g_off // tn
    nd = D // tn

    def yspec(y):
        return pl.BlockSpec((tm, y.shape[1]), lambda i, j: (i, 0))

    def wspec(w):
        return pl.BlockSpec((w.shape[0], tn), lambda i, j: (0, j))

    def gspec(k):
        return pl.BlockSpec((tm, tn), lambda i, j: (i, g0 + k * nd + j))

    return pl.pallas_call(
        _merge_kernel,
        grid=(R // tm, nd),
        in_specs=[yspec(ya), yspec(yb), yspec(yc), wspec(wa), wspec(wb), wspec(wc), gspec(0), gspec(1), gspec(2)],
        out_specs=pl.BlockSpec((tm, tn), lambda i, j: (i, j)),
        out_shape=jax.ShapeDtypeStruct((R, D), BF16),
        compiler_params=_cp("arbitrary", "arbitrary"),
        name="merge",
    )(ya, yb, yc, wa, wb, wc, proj, proj, proj)


def _post1_kernel(mo_ref, x_ref, bo_ref, g1x, g1c, sh2x, sh2c, sc2x, sc2c, lg_ref, lb_ref, rw_ref, rb_ref,
                  x1_ref, h2_ref, idx_ref, wt_ref, *, nC, alpha, E):
    is_ctx = pl.program_id(1) < nC
    g1 = jnp.where(is_ctx, g1c[...], g1x[...])
    sh2 = jnp.where(is_ctx, sh2c[...], sh2x[...])
    sc2 = jnp.where(is_ctx, sc2c[...], sc2x[...])
    y = alpha * x_ref[...] + g1 * (mo_ref[...].astype(F32) + bo_ref[...])
    x1 = _ln(y) * lg_ref[...] + lb_ref[...]
    x1_ref[...] = x1
    h2 = _ln(x1) * (1.0 + sc2) + sh2
    h2_ref[...] = h2
    logits = jnp.dot(h2, rw_ref[...], preferred_element_type=F32, precision=lax.Precision.HIGHEST) + rb_ref[...]
    col = lax.broadcasted_iota(jnp.int32, logits.shape, 1)
    lane = lax.broadcasted_iota(jnp.int32, (logits.shape[0], LANES), 1)
    idx_out = jnp.zeros((logits.shape[0], LANES), jnp.int32)
    val_out = jnp.zeros((logits.shape[0], LANES), F32)
    lg = logits
    v0 = None
    den = 0.0
    for k in range(TOP_K):
        m = jnp.max(lg, axis=1, keepdims=True)
        am = jnp.min(jnp.where(lg == m, col, E), axis=1, keepdims=True)
        lg = jnp.where(col == am, -jnp.inf, lg)
        if k == 0:
            v0 = m
        e = jnp.exp(m - v0)
        den = den + e
        idx_out = jnp.where(lane == k, am, idx_out)
        val_out = jnp.where(lane == k, e, val_out)
    idx_ref[...] = idx_out
    wt_ref[...] = val_out / den


def _post1(mo, xall, b_out, mod3, ln_g, ln_b, rw, rb, B, S, C, alpha):
    R, D = xall.shape
    E = rw.shape[1]
    Tb = C + S
    tr = 256
    nC = C // tr
    nT = Tb // tr
    bi = lambda b, j: b
    ci = lambda b, j: B
    row = pl.BlockSpec((tr, D), lambda b, j: (b * nT + j, 0))
    vec = pl.BlockSpec((1, D), lambda b, j: (0, 0))
    small = pl.BlockSpec((tr, LANES), lambda b, j: (b * nT + j, 0))
    return pl.pallas_call(
        functools.partial(_post1_kernel, nC=nC, alpha=alpha, E=E),
        grid=(B, nT),
        in_specs=[row, row, vec,
                  _mod_spec(D, bi, 2), _mod_spec(D, ci, 2), _mod_spec(D, bi, 3), _mod_spec(D, ci, 3),
                  _mod_spec(D, bi, 4), _mod_spec(D, ci, 4), vec, vec,
                  pl.BlockSpec((D, E), lambda b, j: (0, 0)), pl.BlockSpec((1, E), lambda b, j: (0, 0))],
        out_specs=[row, row, small, small],
        out_shape=[jax.ShapeDtypeStruct((R, D), F32), jax.ShapeDtypeStruct((R, D), F32),
                   jax.ShapeDtypeStruct((R, LANES), jnp.int32), jax.ShapeDtypeStruct((R, LANES), F32)],
        compiler_params=_cp("arbitrary", "arbitrary"),
        name="post1",
    )(mo, xall, b_out.reshape(1, D), mod3, mod3, mod3, mod3, mod3, mod3,
      ln_g.reshape(1, D), ln_b.reshape(1, D), rw, rb.reshape(1, E))


def _moe_kernel(te_ref, tj0_ref, tok_ref, nact_ref, h_hbm, wg_ref, bg_ref, wu_ref, bu_ref, wd_ref, bd_ref,
                y_ref, buf, sem, *, tm):
    t = pl.program_id(0)
    nact = nact_ref[0]

    def row_copy(j0, i, slot):
        return pltpu.make_async_copy(h_hbm.at[pl.ds(tok_ref[j0 + i], 1), :], buf.at[slot, pl.ds(i, 1), :],
                                     sem.at[slot])

    def wait_tile(slot):
        pltpu.make_async_copy(h_hbm.at[pl.ds(0, tm), :], buf.at[slot], sem.at[slot]).wait()

    def gather(tile, slot):
        j0 = tj0_ref[tile]

        def issue(i, carry):
            row_copy(j0, i, slot).start()
            return carry

        lax.fori_loop(0, tm, issue, 0, unroll=8)

    @pl.when(t == 0)
    def _():
        gather(0, 0)

    @pl.when(t + 1 < nact)
    def _():
        gather(t + 1, (t + 1) % 2)

    @pl.when(t < nact)
    def _():
        slot = t % 2
        wait_tile(slot)
        xb = buf[slot].astype(BF16)
        gate = jnp.minimum(jnp.dot(xb, wg_ref[...], preferred_element_type=F32) + bg_ref[...], SWIGLU_LIMIT)
        up = jnp.clip(jnp.dot(xb, wu_ref[...], preferred_element_type=F32) + bu_ref[...], -SWIGLU_LIMIT, SWIGLU_LIMIT)
        act = ((up + 1.0) * gate * _sigmoid(SWIGLU_ALPHA * gate)).astype(BF16)
        y_ref[...] = jnp.dot(act, wd_ref[...], preferred_element_type=F32) + bd_ref[...]

    @pl.when(t >= nact)
    def _():
        y_ref[...] = jnp.zeros_like(y_ref)


def _moe(h2, te, tj0, tok, nact, wg, bg, wu, bu, wd, bd, l, tm, n_tiles):
    R, D = h2.shape
    F = wg.shape[2]
    em = lambda t, te, tj0, tok, nact: (te[t], 0, 0)
    eb = lambda t, te, tj0, tok, nact: (l, te[t], 0, 0)
    grid_spec = pltpu.PrefetchScalarGridSpec(
        num_scalar_prefetch=4,
        grid=(n_tiles,),
        in_specs=[pl.BlockSpec(memory_space=pl.ANY),
                  pl.BlockSpec((None, D, F), em), pl.BlockSpec((None, None, 1, F), eb),
                  pl.BlockSpec((None, D, F), em), pl.BlockSpec((None, None, 1, F), eb),
                  pl.BlockSpec((None, F, D), em), pl.BlockSpec((None, None, 1, D), eb)],
        out_specs=pl.BlockSpec((tm, D), lambda t, te, tj0, tok, nact: (t, 0)),
        scratch_shapes=[pltpu.VMEM((2, tm, D), F32), pltpu.SemaphoreType.DMA((2,))],
    )
    return pl.pallas_call(
        functools.partial(_moe_kernel, tm=tm),
        grid_spec=grid_spec,
        out_shape=jax.ShapeDtypeStruct((n_tiles * tm, D), F32),
        compiler_params=_cp("arbitrary"),
        name="moe",
    )(te, tj0, tok, nact, h2, wg, bg, wu, bu, wd, bd)


def _dispatch(idx, R, E, tm, n_tiles):
    e_flat = idx.reshape(-1)
    n_pairs = e_flat.shape[0]
    oh = e_flat[:, None] == jnp.arange(E, dtype=jnp.int32)[None, :]
    cs = jnp.cumsum(oh.astype(jnp.int32), axis=0)
    rank = jnp.sum(jnp.where(oh, cs, 0), axis=1) - 1
    counts = cs[-1]
    tiles_e = (counts + tm - 1) // tm
    tile_end = jnp.cumsum(tiles_e)
    tile_start = tile_end - tiles_e
    nact = tile_end[-1]
    cum_start = jnp.cumsum(counts) - counts
    dest = jnp.sum(jnp.where(oh, tile_start[None, :], 0), axis=1) * tm + rank
    tok = jnp.arange(n_pairs, dtype=jnp.int32) // TOP_K
    sorted_tok = jnp.concatenate([jnp.sort(e_flat * R + tok) % R, jnp.zeros((tm,), jnp.int32)])
    tt = jnp.minimum(jnp.arange(n_tiles, dtype=jnp.int32), nact - 1)
    te = jnp.sum((tt[:, None] >= tile_end[None, :]).astype(jnp.int32), axis=1)
    te = jnp.minimum(te, E - 1)
    tj0 = cum_start[te] + (tt - tile_start[te]) * tm
    return (te.astype(jnp.int32), tj0.astype(jnp.int32), sorted_tok.astype(jnp.int32),
            nact.reshape(1).astype(jnp.int32), dest.astype(jnp.int32))


def _combine_kernel(dest_ref, y_hbm, wt_ref, x1_ref, g2x, g2c, sh1x, sh1c, sc1x, sc1c, lg_ref, lb_ref,
                    *rest, tc, nT, j_off, nC, alpha, last):
    if last:
        out_ref, buf, sem = rest
    else:
        xo_ref, h_ref, buf, sem = rest
    nJ = nT - j_off
    step = pl.program_id(0) * nJ + pl.program_id(1)
    n_steps = pl.num_programs(0) * nJ
    j = pl.program_id(1) + j_off

    def row_copy(p0, i, k, slot):
        return pltpu.make_async_copy(y_hbm.at[pl.ds(dest_ref[p0 + i * TOP_K + k], 1), :],
                                     buf.at[slot, k, pl.ds(i, 1), :], sem.at[slot])

    def pair0(s):
        return ((s // nJ) * nT + (s % nJ) + j_off) * (tc * TOP_K)

    def wait_step(slot):
        for k in range(TOP_K):
            pltpu.make_async_copy(y_hbm.at[pl.ds(0, tc), :], buf.at[slot, k], sem.at[slot]).wait()

    def gather(s, slot):
        p0 = pair0(s)

        def issue(i, carry):
            for k in range(TOP_K):
                row_copy(p0, i, k, slot).start()
            return carry

        lax.fori_loop(0, tc, issue, 0, unroll=4)

    @pl.when(step == 0)
    def _():
        gather(0, 0)

    @pl.when(step + 1 < n_steps)
    def _():
        gather(step + 1, (step + 1) % 2)

    slot = step % 2
    wait_step(slot)
    wt = wt_ref[...]
    f = wt[:, 0:1] * buf[slot, 0]
    for k in range(1, TOP_K):
        f = f + wt[:, k:k + 1] * buf[slot, k]
    is_ctx = j < nC
    g2 = jnp.where(is_ctx, g2c[...], g2x[...])
    x2 = _ln(alpha * x1_ref[...] + g2 * f) * lg_ref[...] + lb_ref[...]
    if last:
        out_ref[...] = x2
    else:
        sh1 = jnp.where(is_ctx, sh1c[...], sh1x[...])
        sc1 = jnp.where(is_ctx, sc1c[...], sc1x[...])
        xo_ref[...] = x2
        h_ref[...] = (_ln(x2) * (1.0 + sc1) + sh1).astype(BF16)


def _combine(ys, dest, wts, x1, mod3, mod3_next, ln_g, ln_b, B, S, C, alpha, last):
    R, D = x1.shape
    Tb = C + S
    tc = 128
    nC = C // tc
    nT = Tb // tc
    j_off = nC if last else 0
    nJ = nT - j_off
    bi = lambda b, j, d: b
    ci = lambda b, j, d: B
    row = pl.BlockSpec((tc, D), lambda b, j, d: (b * nT + j + j_off, 0))
    vec = pl.BlockSpec((1, D), lambda b, j, d: (0, 0))
    if last:
        out_specs = pl.BlockSpec((tc, D), lambda b, j, d: (b * nJ + j, 0))
        out_shape = jax.ShapeDtypeStruct((B * S, D), F32)
    else:
        out_specs = [row, row]
        out_shape = [jax.ShapeDtypeStruct((R, D), F32), jax.ShapeDtypeStruct((R, D), BF16)]
    grid_spec = pltpu.PrefetchScalarGridSpec(
        num_scalar_prefetch=1,
        grid=(B, nJ),
        in_specs=[pl.BlockSpec(memory_space=pl.ANY),
                  pl.BlockSpec((tc, LANES), lambda b, j, d: (b * nT + j + j_off, 0)),
                  row,
                  _mod_spec(D, bi, 5), _mod_spec(D, ci, 5), _mod_spec(D, bi, 0), _mod_spec(D, ci, 0),
                  _mod_spec(D, bi, 1), _mod_spec(D, ci, 1), vec, vec],
        out_specs=out_specs,
        scratch_shapes=[pltpu.VMEM((2, TOP_K, tc, D), F32), pltpu.SemaphoreType.DMA((2,))],
    )
    return pl.pallas_call(
        functools.partial(_combine_kernel, tc=tc, nT=nT, j_off=j_off, nC=nC, alpha=alpha, last=last),
        grid_spec=grid_spec,
        out_shape=out_shape,
        compiler_params=_cp("arbitrary", "arbitrary"),
        name="combine",
    )(dest, ys, wts, x1, mod3, mod3, mod3_next, mod3_next, mod3_next, mod3_next,
      ln_g.reshape(1, D), ln_b.reshape(1, D))


def _rope_tables(S, C, head_dim):
    rows = S // GRID_W
    pos_row = jnp.repeat(jnp.arange(rows, dtype=jnp.int32), GRID_W).astype(F32)
    pos_col = jnp.tile(jnp.arange(GRID_W, dtype=jnp.int32), rows).astype(F32)
    n = head_dim // 4
    inv = ROPE_BASE ** (-jnp.arange(n, dtype=F32) / n)
    ang = jnp.concatenate([pos_row[:, None] * inv, pos_col[:, None] * inv], -1)
    cos, sin = jnp.cos(ang), jnp.sin(ang)
    reps = LANES // head_dim
    cos_f = jnp.tile(jnp.concatenate([cos, cos], -1), (1, reps))
    sin_f = jnp.tile(jnp.concatenate([-sin, sin], -1), (1, reps))
    cos_f = jnp.concatenate([jnp.ones((C, LANES), F32), cos_f], 0)
    sin_f = jnp.concatenate([jnp.zeros((C, LANES), F32), sin_f], 0)
    return cos_f, sin_f


def kernel(x, c, ctx, c_ctx, w_ada, b_ada, w_in, conv_w, conv_b, lru_wr, lru_br, lru_wi, lru_bi, lru_lambda, wa_sink, da_lq1, da_lk1, da_lq2, da_lk2, da_norm_g, w_br_a, w_br_b, w_br_c, w_out, b_out, ln1_g, ln1_b, router_w, router_b, exp_w_gate, exp_b_gate, exp_w_up, exp_b_up, exp_w_down, exp_b_down, ln2_g, ln2_b):
    B, S, D = x.shape
    C = ctx.shape[1]
    depth = w_ada.shape[0]
    Tb = C + S
    R = B * Tb
    d_in = w_in.shape[2]
    d_rnn = w_br_a.shape[1]
    wa_w = w_br_b.shape[1]
    da_w = w_br_c.shape[1]
    kv_w = (d_in - d_rnn - wa_w - 3 * da_w - 3 * D) // 2
    wa_heads = wa_w // WA_HEAD_DIM
    kvh_n = kv_w // WA_HEAD_DIM
    G = wa_heads // kvh_n
    da_heads = da_w // (2 * DA_HEAD_DIM)
    nrb = lru_wr.shape[2]
    E = router_w.shape[2]
    alpha = (2 * depth) ** 0.25
    assert B + 1 <= 8 and lru_wr.shape[3] == LANES and S % GRID_W == 0
    q_off = d_rnn
    k_off = q_off + wa_w
    v_off = k_off + kv_w
    qd_off = v_off + kv_w
    kd_off = qd_off + da_w
    vd_off = kd_off + da_w
    g_off = vd_off + da_w

    cos_wa, sin_wa = _rope_tables(S, C, WA_HEAD_DIM)
    cos_da, sin_da = _rope_tables(S, C, DA_HEAD_DIM)

    cc = jnp.concatenate([c, c_ctx[None, :], jnp.zeros((8 - B - 1, D), F32)], axis=0)
    b_ada3 = b_ada.reshape(depth, 1, 6 * D)
    mods = [_ada(cc, w_ada, b_ada3, l).reshape(8, 1, 6 * D) for l in range(depth)]

    xall, h = _ln0(x.reshape(B * S, D), ctx.reshape(B * C, D), mods[0], B, S, C)

    F = exp_w_gate.shape[3]
    wg3 = exp_w_gate.reshape(depth, E * D, F)
    wu3 = exp_w_up.reshape(depth, E * D, F)
    wd3 = exp_w_down.reshape(depth, E * F, D)
    bg_e = exp_b_gate.reshape(depth, E, 1, F)
    bu_e = exp_b_up.reshape(depth, E, 1, F)
    bd_e = exp_b_down.reshape(depth, E, 1, D)
    conv_b3 = conv_b.reshape(depth, 1, d_rnn)

    tm_moe = 256
    n_pairs = R * TOP_K
    n_tiles = n_pairs // tm_moe + E

    out = None
    for l in range(depth):
        last = l == depth - 1
        lam_init = 0.8 - 0.6 * math.exp(-0.3 * l)
        proj, (wg_b, wu_b, wd_b, wa_b, wb_b, wc_b) = _mm(
            h, w_in, l, 1088, 512,
            side=((wg3, E * D // 256), (wu3, E * D // 256), (wd3, E * F // 256),
                  (w_br_a, 16), (w_br_b, 16), (w_br_c, 16)))
        wg = jnp.concatenate([lru_wr[l, 0], lru_wi[l, 0], lru_wr[l, 1], lru_wi[l, 1]], axis=-1).astype(BF16)
        bg = jnp.stack([lru_br[l, 0].reshape(nrb, LANES), lru_bi[l, 0].reshape(nrb, LANES),
                        lru_br[l, 1].reshape(nrb, LANES), lru_bi[l, 1].reshape(nrb, LANES)], axis=1)
        bg = bg.reshape(nrb, 1, 4 * LANES)
        lam2 = jnp.stack([lru_lambda[l, 0].reshape(nrb, LANES), lru_lambda[l, 1].reshape(nrb, LANES)], axis=1)
        lam2 = lam2.reshape(nrb, 1, 2 * LANES)
        ya = _rglru(proj, conv_w, conv_b3, wg, bg, lam2, l, B, S, C, d_rnn)
        yb = _wattn(proj, wa_sink[l], cos_wa, sin_wa, B, S, C, q_off, k_off, v_off, kvh_n, G)
        lqk = jnp.stack([da_lq1[l], da_lk1[l], da_lq2[l], da_lk2[l]], axis=0)
        yc = _dattn(proj, cos_da, sin_da, da_norm_g[l].reshape(1, LANES), lqk, B, S, C,
                    qd_off, kd_off, vd_off, da_heads, lam_init)
        m = _merge(ya, yb, yc, wa_b, wb_b, wc_b, proj, g_off, D)
        mo, _ = _mm(m, w_out, l, 1088, 512)
        x1, h2, idx, wts = _post1(mo, xall, b_out[l], mods[l], ln1_g[l], ln1_b[l], router_w[l], router_b[l],
                                  B, S, C, alpha)
        te, tj0, tok, nact, dest = _dispatch(idx[:, :TOP_K], R, E, tm_moe, n_tiles)
        ys = _moe(h2, te, tj0, tok, nact, wg_b.reshape(E, D, F), bg_e, wu_b.reshape(E, D, F), bu_e,
                  wd_b.reshape(E, F, D), bd_e, l, tm_moe, n_tiles)
        nxt = mods[l + 1] if not last else mods[l]
        res = _combine(ys, dest, wts, x1, mods[l], nxt, ln2_g[l], ln2_b[l], B, S, C, alpha, last)
        if last:
            out = res.reshape(B, S, D)
        else:
            xall, h = res
    return out
```

```python
import functools
import math

import jax
import jax.numpy as jnp
from jax import lax
from jax.experimental import pallas as pl
from jax.experimental.pallas import tpu as pltpu

F32 = jnp.float32
BF16 = jnp.bfloat16

LN_EPS = 1e-5
GRID_W = 64
ROPE_BASE = 10000.0
LRU_C = 8.0
WA_HEAD_DIM = 128
DA_HEAD_DIM = 64
WINDOW = 128
TOP_K = 4
SWIGLU_LIMIT = 7.0
SWIGLU_ALPHA = 1.702
LANES = 128
SUBLANES = 8
NEG = -1e30
LOG2E = 1.4426950408889634
DA_CHUNKS = 17
ONES_ROWS = 16
VMEM_LIMIT = 56 * 1024 * 1024


def _cp(*sem):
    return pltpu.CompilerParams(dimension_semantics=sem, vmem_limit_bytes=VMEM_LIMIT)


def _sigmoid(v):
    return 1.0 / (1.0 + jnp.exp(-v))


def _ln(v):
    mu = jnp.mean(v, axis=-1, keepdims=True)
    d = v - mu
    var = jnp.mean(d * d, axis=-1, keepdims=True)
    return d * lax.rsqrt(var + LN_EPS)


def _ada_kernel(c_ref, w_ref, b_ref, o_ref):
    c = c_ref[...]
    s = (c * _sigmoid(c)).astype(BF16)
    o_ref[...] = jnp.dot(s, w_ref[...].astype(BF16), preferred_element_type=F32) + b_ref[...]


def _ada(cc, w, b, l):
    _, D, N = w.shape
    tn = 512
    return pl.pallas_call(
        _ada_kernel,
        grid=(N // tn,),
        in_specs=[pl.BlockSpec((8, D), lambda j: (0, 0)),
                  pl.BlockSpec((None, D, tn), lambda j: (l, 0, j)),
                  pl.BlockSpec((None, 1, tn), lambda j: (l, 0, j))],
        out_specs=pl.BlockSpec((8, tn), lambda j: (0, j)),
        out_shape=jax.ShapeDtypeStruct((8, N), F32),
        compiler_params=_cp("arbitrary"),
        name="ada",
    )(cc, w, b)


def _mod_spec(D, row_fn, chunk):
    return pl.BlockSpec((None, 1, D), lambda *a: (row_fn(*a), 0, chunk))


def _ln0_kernel(x_ref, c_ref, shx, scx, shc, scc, xo_ref, h_ref, *, nC):
    is_ctx = pl.program_id(1) < nC
    v = jnp.where(is_ctx, c_ref[...], x_ref[...])
    sh = jnp.where(is_ctx, shc[...], shx[...])
    sc = jnp.where(is_ctx, scc[...], scx[...])
    xo_ref[...] = v
    h_ref[...] = (_ln(v) * (1.0 + sc) + sh).astype(BF16)


def _ln0(x2, ctx2, mod3, B, S, C):
    D = x2.shape[1]
    Tb = C + S
    tr = 256
    nC = C // tr
    nS = S // tr
    nT = Tb // tr
    bi = lambda b, j: b
    ci = lambda b, j: B
    return pl.pallas_call(
        functools.partial(_ln0_kernel, nC=nC),
        grid=(B, nT),
        in_specs=[pl.BlockSpec((tr, D), lambda b, j: (b * nS + jnp.maximum(j - nC, 0), 0)),
                  pl.BlockSpec((tr, D), lambda b, j: (b * nC + jnp.minimum(j, nC - 1), 0)),
                  _mod_spec(D, bi, 0), _mod_spec(D, bi, 1), _mod_spec(D, ci, 0), _mod_spec(D, ci, 1)],
        out_specs=[pl.BlockSpec((tr, D), lambda b, j: (b * nT + j, 0)),
                   pl.BlockSpec((tr, D), lambda b, j: (b * nT + j, 0))],
        out_shape=[jax.ShapeDtypeStruct((B * Tb, D), F32), jax.ShapeDtypeStruct((B * Tb, D), BF16)],
        compiler_params=_cp("arbitrary", "arbitrary"),
        name="ln0",
    )(x2, ctx2, mod3, mod3, mod3, mod3)


def _mm_kernel(a_ref, w_ref, *rest, n_side):
    o_ref = rest[n_side]
    o_ref[...] = jnp.dot(a_ref[...], w_ref[...].astype(BF16), preferred_element_type=F32).astype(o_ref.dtype)
    for src, dst in zip(rest[:n_side], rest[n_side + 1:]):
        dst[...] = src[...].astype(BF16)


def _mm(a, w, l, tm, tn, side=()):
    M, K = a.shape
    N = w.shape[2]
    nj = N // tn
    in_specs = [pl.BlockSpec((tm, K), lambda i, j: (i, 0)),
                pl.BlockSpec((None, K, tn), lambda i, j: (l, 0, j))]
    out_specs = [pl.BlockSpec((tm, tn), lambda i, j: (i, j))]
    out_shape = [jax.ShapeDtypeStruct((M, N), BF16)]
    for arr, br in side:
        _, rows, cols = arr.shape
        last = rows // br - 1
        assert rows % br == 0 and last < (M // tm) * nj
        in_specs.append(pl.BlockSpec((None, br, cols), lambda i, j, last=last: (l, jnp.minimum(i * nj + j, last), 0)))
        out_specs.append(pl.BlockSpec((br, cols), lambda i, j, last=last: (jnp.minimum(i * nj + j, last), 0)))
        out_shape.append(jax.ShapeDtypeStruct((rows, cols), BF16))
    res = pl.pallas_call(
        functools.partial(_mm_kernel, n_side=len(side)),
        grid=(M // tm, nj),
        in_specs=in_specs,
        out_specs=out_specs,
        out_shape=out_shape,
        compiler_params=_cp("arbitrary", "arbitrary"),
        name="mm",
    )(a, w, *[arr for arr, _ in side])
    return res[0], res[1:]


def _rglru_kernel(u_ref, cw_ref, cb_ref, wg_ref, bg_ref, lam_ref, y_ref, h_scr, a_scr, o_scr, *, C, S):
    T = C + S
    nv = T // SUBLANES
    u = u_ref[...].astype(F32)
    cw = cw_ref[...]
    cb = cb_ref[...]
    sub8 = lax.broadcasted_iota(jnp.int32, (SUBLANES, LANES), 0)

    def conv_seq(us):
        n = us.shape[0]
        t_m1 = pltpu.roll(us, 1, 0)
        t_m1 = jnp.concatenate([jnp.where(sub8 < 1, 0.0, t_m1[0:SUBLANES]), t_m1[SUBLANES:]], axis=0)
        t_p1 = pltpu.roll(us, n - 1, 0)
        t_p1 = jnp.concatenate([t_p1[:n - SUBLANES], jnp.where(sub8 >= SUBLANES - 1, 0.0, t_p1[n - SUBLANES:])],
                               axis=0)
        t_p2 = pltpu.roll(us, n - 2, 0)
        t_p2 = jnp.concatenate([t_p2[:n - SUBLANES], jnp.where(sub8 >= SUBLANES - 2, 0.0, t_p2[n - SUBLANES:])],
                               axis=0)
        return cb + cw[0:1] * t_m1 + cw[1:2] * us + cw[2:3] * t_p1 + cw[3:4] * t_p2

    v = jnp.concatenate([conv_seq(u[0:C]), conv_seq(u[C:T])], axis=0)
    g = jnp.dot(v.astype(BF16), wg_ref[...], preferred_element_type=F32) + bg_ref[...]
    lam = lam_ref[...]
    nl = -lam
    sp = jnp.maximum(nl, 0.0) + jnp.log1p(jnp.exp(-jnp.abs(nl)))

    for d in range(2):
        r = 0.5 * jnp.tanh(0.5 * g[:, (2 * d) * LANES:(2 * d + 1) * LANES]) + 0.5
        i = 0.5 * jnp.tanh(0.5 * g[:, (2 * d + 1) * LANES:(2 * d + 2) * LANES]) + 0.5
        a = jnp.exp((-LRU_C) * sp[:, d * LANES:(d + 1) * LANES] * r)
        a_scr[d] = a
        h_scr[d] = jnp.sqrt((1.0 - a) * (1.0 + a)) * (i * v)

    nvc = C // SUBLANES

    def scan_group(d, k):
        rows = pl.ds(pl.multiple_of(k * SUBLANES, SUBLANES), SUBLANES)
        a8 = a_scr[d, rows, :]
        b8 = h_scr[d, rows, :]
        for s in (1, 2, 4):
            if d == 0:
                ok = sub8 >= s
                shift = s
            else:
                ok = sub8 < SUBLANES - s
                shift = SUBLANES - s
            a_sh = jnp.where(ok, pltpu.roll(a8, shift, 0), 1.0)
            b_sh = jnp.where(ok, pltpu.roll(b8, shift, 0), 0.0)
            b8 = a8 * b_sh + b8
            a8 = a8 * a_sh
        return b8, a8

    def rev_group(k):
        return jnp.where(k < nvc, nvc - 1 - k, nv - 1 - (k - nvc))

    def step(k, carry):
        cf, cr, hf, pf, hr, pr = carry
        of = hf + pf * cf
        o_scr[0, pl.ds(pl.multiple_of(k * SUBLANES, SUBLANES), SUBLANES), :] = of
        cf = jnp.broadcast_to(of[SUBLANES - 1:SUBLANES, :], (SUBLANES, LANES))
        orv = hr + pr * cr
        o_scr[1, pl.ds(pl.multiple_of(rev_group(k) * SUBLANES, SUBLANES), SUBLANES), :] = orv
        cr = jnp.broadcast_to(orv[0:1, :], (SUBLANES, LANES))
        kn = jnp.minimum(k + 1, nv - 1)
        hf, pf = scan_group(0, kn)
        hr, pr = scan_group(1, rev_group(kn))
        return cf, cr, hf, pf, hr, pr

    z = jnp.zeros((SUBLANES, LANES), F32)
    lax.fori_loop(0, nv, step, (z, z) + scan_group(0, 0) + scan_group(1, rev_group(0)), unroll=8)
    y_ref[...] = (o_scr[0] + o_scr[1]).astype(BF16)


def _rglru(proj, conv_w, conv_b, wg, bg, lam2, l, B, S, C, d_rnn):
    Tb = C + S
    nb = d_rnn // LANES
    return pl.pallas_call(
        functools.partial(_rglru_kernel, C=C, S=S),
        grid=(B, nb),
        in_specs=[pl.BlockSpec((Tb, LANES), lambda b, n: (b, n)),
                  pl.BlockSpec((None, 4, LANES), lambda b, n: (l, 0, n)),
                  pl.BlockSpec((None, 1, LANES), lambda b, n: (l, 0, n)),
                  pl.BlockSpec((None, LANES, 4 * LANES), lambda b, n: (n, 0, 0)),
                  pl.BlockSpec((None, 1, 4 * LANES), lambda b, n: (n, 0, 0)),
                  pl.BlockSpec((None, 1, 2 * LANES), lambda b, n: (n, 0, 0))],
        out_specs=pl.BlockSpec((Tb, LANES), lambda b, n: (b, n)),
        out_shape=jax.ShapeDtypeStruct((B * Tb, d_rnn), BF16),
        scratch_shapes=[pltpu.VMEM((2, Tb, LANES), F32), pltpu.VMEM((2, Tb, LANES), F32),
                        pltpu.VMEM((2, Tb, LANES), F32)],
        compiler_params=_cp("arbitrary", "arbitrary"),
        name="rglru",
    )(proj, conv_w, conv_b, wg, bg, lam2)


def _wattn_kernel(sink_ref, q_ref, k_ref, v_ref, cos_ref, sin_ref, o_ref, kr_scr, vt_scr, *, C, Tb, G):
    kvh = pl.program_id(1)
    j = pl.program_id(2)
    W = WINDOW
    nblk = Tb // W
    ncb = C // W
    scale = WA_HEAD_DIM ** -0.5
    nt = (((1,), (1,)), ((), ()))

    @pl.when(j == 0)
    def _():
        kf = k_ref[...].astype(F32)
        kr = kf * cos_ref[...] + pltpu.roll(kf, WA_HEAD_DIM // 2, 1) * sin_ref[...]
        zk = jnp.zeros((W, WA_HEAD_DIM), BF16)
        kr_scr[0:W, :] = zk
        kr_scr[W:W + Tb, :] = kr.astype(BF16)
        kr_scr[W + Tb:2 * W + Tb, :] = zk
        vt = v_ref[...].astype(F32).T.astype(BF16)
        vt_scr[0] = zk
        vt_scr[nblk + 1] = zk
        for bk in range(nblk):
            vt_scr[bk + 1] = vt[:, bk * W:(bk + 1) * W]

    r0 = pl.multiple_of(j * W, W)
    cs = cos_ref[pl.ds(r0, W), :]
    sn = sin_ref[pl.ds(r0, W), :]
    qs = []
    for g in range(G):
        qg = q_ref[:, g * WA_HEAD_DIM:(g + 1) * WA_HEAD_DIM].astype(F32)
        qs.append((qg * cs + pltpu.roll(qg, WA_HEAD_DIM // 2, 1) * sn) * scale)
    qq = jnp.concatenate(qs, axis=0).astype(BF16)
    sl = lax.dot_general(kr_scr[pl.ds(r0, 3 * W), :], qq, nt, preferred_element_type=F32)
    sx = lax.dot_general(kr_scr[W:W + C, :], qq, nt, preferred_element_type=F32)
    ki = lax.broadcasted_iota(jnp.int32, (3 * W, W), 0)
    qi = lax.broadcasted_iota(jnp.int32, (3 * W, W), 1)
    lo = C - (j - 1) * W
    hi = Tb - (j - 1) * W
    valid = (ki >= qi) & (ki <= qi + 2 * W) & (ki >= lo) & (ki < hi) & (j >= ncb)
    sl = jnp.concatenate([jnp.where(valid, sl[:, g * W:(g + 1) * W], NEG) for g in range(G)], axis=1)
    lane = lax.broadcasted_iota(jnp.int32, (1, G * W), 1)
    sink = jnp.zeros((1, G * W), F32)
    for g in range(G):
        sink = jnp.where(lane // W == g, sink_ref[kvh * G + g], sink)
    m = jnp.maximum(jnp.maximum(jnp.max(sl, axis=0, keepdims=True), jnp.max(sx, axis=0, keepdims=True)), sink)
    el = jnp.exp(sl - m)
    ex = jnp.exp(sx - m)
    l = jnp.sum(el, axis=0, keepdims=True) + jnp.sum(ex, axis=0, keepdims=True) + jnp.exp(sink - m)
    elb = el.astype(BF16)
    exb = ex.astype(BF16)
    ot = jnp.dot(vt_scr[j], elb[0:W], preferred_element_type=F32)
    ot = ot + jnp.dot(vt_scr[j + 1], elb[W:2 * W], preferred_element_type=F32)
    ot = ot + jnp.dot(vt_scr[j + 2], elb[2 * W:3 * W], preferred_element_type=F32)
    for cb in range(ncb):
        ot = ot + jnp.dot(vt_scr[1 + cb], exb[cb * W:(cb + 1) * W], preferred_element_type=F32)
    ot = ot * (1.0 / l)
    for g in range(G):
        o_ref[:, g * WA_HEAD_DIM:(g + 1) * WA_HEAD_DIM] = ot[:, g * W:(g + 1) * W].T.astype(BF16)


def _wattn(proj, sink, cos, sin, B, S, C, q_off, k_off, v_off, kvh_n, G):
    Tb = C + S
    nblk = Tb // WINDOW
    qw = G * WA_HEAD_DIM
    qc0 = q_off // qw
    kc0 = k_off // WA_HEAD_DIM
    vc0 = v_off // WA_HEAD_DIM
    assert C % WINDOW == 0 and WINDOW == WA_HEAD_DIM == LANES

    def slab(col0):
        return pl.BlockSpec((Tb, WA_HEAD_DIM), lambda b, h, j: (b, col0 + h))

    full = pl.BlockSpec((Tb, WA_HEAD_DIM), lambda b, h, j: (0, 0))
    return pl.pallas_call(
        functools.partial(_wattn_kernel, C=C, Tb=Tb, G=G),
        grid=(B, kvh_n, nblk),
        in_specs=[pl.BlockSpec(memory_space=pltpu.SMEM),
                  pl.BlockSpec((WINDOW, qw), lambda b, h, j: (b * nblk + j, qc0 + h)),
                  slab(kc0), slab(vc0), full, full],
        out_specs=pl.BlockSpec((WINDOW, qw), lambda b, h, j: (b * nblk + j, h)),
        out_shape=jax.ShapeDtypeStruct((B * Tb, kvh_n * qw), BF16),
        scratch_shapes=[pltpu.VMEM((Tb + 2 * WINDOW, WA_HEAD_DIM), BF16),
                        pltpu.VMEM((nblk + 2, WA_HEAD_DIM, WINDOW), BF16)],
        compiler_params=_cp("arbitrary", "arbitrary", "arbitrary"),
        name="wattn",
    )(sink, proj, proj, proj, cos, sin)


def _dattn_kernel(q_ref, k_ref, v_ref, cos_ref, sin_ref, g_ref, lqk_ref, o_ref, kr_scr, vt_scr, s0_scr, s1_scr,
                  *, C, S, tq, lam_init):
    Tb = C + S
    half = DA_HEAD_DIM // 2
    lane = lax.broadcasted_iota(jnp.int32, (1, LANES), 1)
    lo = (lane % DA_HEAD_DIM) < half
    nt = (((1,), (1,)), ((), ()))

    def rope(xv, cs, sn):
        rot = jnp.where(lo, pltpu.roll(xv, LANES - half, 1), pltpu.roll(xv, half, 1))
        return xv * cs + rot * sn

    kr_scr[...] = rope(k_ref[...].astype(F32), cos_ref[...], sin_ref[...]).astype(BF16)
    vt_scr[0:LANES, :] = v_ref[...].astype(F32).T.astype(BF16)
    vt_scr[LANES:LANES + ONES_ROWS, :] = jnp.ones((ONES_ROWS, Tb), BF16)
    lq = lqk_ref[...]
    lam = (jnp.exp(jnp.sum(lq[0:1] * lq[1:2], axis=1, keepdims=True))
           - jnp.exp(jnp.sum(lq[2:3] * lq[3:4], axis=1, keepdims=True)) + lam_init)
    gain = g_ref[...] * (1.0 - lam_init)

    def make_qq(r0):
        qv = rope(q_ref[pl.ds(r0, tq), :].astype(F32), cos_ref[pl.ds(r0, tq), :], sin_ref[pl.ds(r0, tq), :]) * (
            DA_HEAD_DIM ** -0.5 * LOG2E)
        return jnp.concatenate([jnp.where(lane < DA_HEAD_DIM, qv, 0.0), jnp.where(lane >= DA_HEAD_DIM, qv, 0.0)],
                               axis=0).astype(BF16)

    def write_out(r0, otl):
        ot = otl[0:LANES]
        l = otl[LANES:LANES + 1]
        o = (ot[:, 0:tq] * (1.0 / l[:, 0:tq]) - ot[:, tq:2 * tq] * (lam / l[:, tq:2 * tq])).T
        o = o * lax.rsqrt(jnp.mean(o * o, axis=1, keepdims=True) + LN_EPS) * gain
        o_ref[pl.ds(r0, tq), :] = o.astype(BF16)

    def scores(r0, s_scr, nk):
        s = lax.dot_general(kr_scr[0:nk, :], make_qq(r0), nt, preferred_element_type=F32)
        s_scr[0:nk, :] = s
        return jnp.max(s, axis=0, keepdims=True)

    def finish(r0, s_scr, m, nk):
        e = jnp.exp2(s_scr[0:nk, :] - m)
        write_out(r0, jnp.dot(vt_scr[:, 0:nk], e.astype(BF16), preferred_element_type=F32))

    for i in range(C // tq):
        finish(i * tq, s0_scr, scores(i * tq, s0_scr, C), C)

    n_lat = S // tq
    m0 = scores(C, s0_scr, Tb)

    units = Tb // tq
    cuts = [round(units * c / DA_CHUNKS) * tq for c in range(DA_CHUNKS + 1)]

    def fused(r_s, s_w, r_f, s_r, m_f):
        qq = make_qq(r_s)
        m_s = ot = None
        for c in range(DA_CHUNKS):
            k0, k1 = cuts[c], cuts[c + 1]
            s = lax.dot_general(kr_scr[k0:k1, :], qq, nt, preferred_element_type=F32)
            s_w[k0:k1, :] = s
            cm = jnp.max(s, axis=0, keepdims=True)
            m_s = cm if c == 0 else jnp.maximum(m_s, cm)
            e = jnp.exp2(s_r[k0:k1, :] - m_f)
            co = jnp.dot(vt_scr[:, k0:k1], e.astype(BF16), preferred_element_type=F32)
            ot = co if c == 0 else ot + co
        write_out(r_f, ot)
        return m_s

    def pair(p, m0):
        r = pl.multiple_of(C + p * (2 * tq), tq)
        m1 = fused(r + tq, s1_scr, r, s0_scr, m0)
        return fused(r + 2 * tq, s0_scr, r + tq, s1_scr, m1)

    m0 = lax.fori_loop(0, n_lat // 2 - 1, pair, m0)
    r = C + (n_lat - 2) * tq
    m1 = fused(r + tq, s1_scr, r, s0_scr, m0)
    finish(r + tq, s1_scr, m1, Tb)


def _dattn(proj, cos, sin, g, lqk, B, S, C, q_off, k_off, v_off, heads, lam_init):
    Tb = C + S
    tq = 256
    assert C % tq == 0 and (S // tq) % 2 == 0 and S // tq >= 2
    qc0, kc0, vc0 = q_off // LANES, k_off // LANES, v_off // LANES
    full = pl.BlockSpec((Tb, LANES), lambda b, h: (0, 0))

    def slab(col0):
        return pl.BlockSpec((Tb, LANES), lambda b, h: (b, col0 + h))

    return pl.pallas_call(
        functools.partial(_dattn_kernel, C=C, S=S, tq=tq, lam_init=lam_init),
        grid=(B, heads),
        in_specs=[slab(qc0), slab(kc0), slab(vc0), full, full,
                  pl.BlockSpec((1, LANES), lambda b, h: (0, 0)),
                  pl.BlockSpec((4, DA_HEAD_DIM), lambda b, h: (0, 0))],
        out_specs=pl.BlockSpec((Tb, LANES), lambda b, h: (b, h)),
        out_shape=jax.ShapeDtypeStruct((B * Tb, heads * LANES), BF16),
        scratch_shapes=[pltpu.VMEM((Tb, LANES), BF16), pltpu.VMEM((LANES + ONES_ROWS, Tb), BF16),
                        pltpu.VMEM((Tb, 2 * tq), F32), pltpu.VMEM((Tb, 2 * tq), F32)],
        compiler_params=_cp("arbitrary", "arbitrary"),
        name="dattn",
    )(proj, proj, proj, cos, sin, g, lqk)


def _merge_kernel(ya_ref, yb_ref, yc_ref, wa_ref, wb_ref, wc_ref, ga_ref, gb_ref, gc_ref, o_ref):
    def br(y_ref, w_ref, g_ref):
        return _sigmoid(g_ref[...].astype(F32)) * jnp.dot(y_ref[...], w_ref[...], preferred_element_type=F32)
    o_ref[...] = (br(ya_ref, wa_ref, ga_ref) + br(yb_ref, wb_ref, gb_ref) + br(yc_ref, wc_ref, gc_ref)).astype(BF16)


def _merge(ya, yb, yc, wa, wb, wc, proj, g_off, D):
    R = ya.shape[0]
    tm, tn = 544, 512
    g0 = g_off // tn
    nd = D // tn

    def yspec(y):
        return pl.BlockSpec((tm, y.shape[1]), lambda i, j: (i, 0))

    def wspec(w):
        return pl.BlockSpec((w.shape[0], tn), lambda i, j: (0, j))

    def gspec(k):
        return pl.BlockSpec((tm, tn), lambda i, j: (i, g0 + k * nd + j))

    return pl.pallas_call(
        _merge_kernel,
        grid=(R // tm, nd),
        in_specs=[yspec(ya), yspec(yb), yspec(yc), wspec(wa), wspec(wb), wspec(wc), gspec(0), gspec(1), gspec(2)],
        out_specs=pl.BlockSpec((tm, tn), lambda i, j: (i, j)),
        out_shape=jax.ShapeDtypeStruct((R, D), BF16),
        compiler_params=_cp("arbitrary", "arbitrary"),
        name="merge",
    )(ya, yb, yc, wa, wb, wc, proj, proj, proj)


def _post1_kernel(mo_ref, x_ref, bo_ref, g1x, g1c, sh2x, sh2c, sc2x, sc2c, lg_ref, lb_ref, rw_ref, rb_ref,
                  x1_ref, h2_ref, idx_ref, wt_ref, *, nC, alpha, E):
    is_ctx = pl.program_id(1) < nC
    g1 = jnp.where(is_ctx, g1c[...], g1x[...])
    sh2 = jnp.where(is_ctx, sh2c[...], sh2x[...])
    sc2 = jnp.where(is_ctx, sc2c[...], sc2x[...])
    y = alpha * x_ref[...] + g1 * (mo_ref[...].astype(F32) + bo_ref[...])
    x1 = _ln(y) * lg_ref[...] + lb_ref[...]
    x1_ref[...] = x1
    h2 = _ln(x1) * (1.0 + sc2) + sh2
    h2_ref[...] = h2
    logits = jnp.dot(h2, rw_ref[...], preferred_element_type=F32, precision=lax.Precision.HIGHEST) + rb_ref[...]
    col = lax.broadcasted_iota(jnp.int32, logits.shape, 1)
    lane = lax.broadcasted_iota(jnp.int32, (logits.shape[0], LANES), 1)
    idx_out = jnp.zeros((logits.shape[0], LANES), jnp.int32)
    val_out = jnp.zeros((logits.shape[0], LANES), F32)
    lg = logits
    v0 = None
    den = 0.0
    for k in range(TOP_K):
        m = jnp.max(lg, axis=1, keepdims=True)
        am = jnp.min(jnp.where(lg == m, col, E), axis=1, keepdims=True)
        lg = jnp.where(col == am, -jnp.inf, lg)
        if k == 0:
            v0 = m
        e = jnp.exp(m - v0)
        den = den + e
        idx_out = jnp.where(lane == k, am, idx_out)
        val_out = jnp.where(lane == k, e, val_out)
    idx_ref[...] = idx_out
    wt_ref[...] = val_out / den


def _post1(mo, xall, b_out, mod3, ln_g, ln_b, rw, rb, B, S, C, alpha):
    R, D = xall.shape
    E = rw.shape[1]
    Tb = C + S
    tr = 256
    nC = C // tr
    nT = Tb // tr
    bi = lambda b, j: b
    ci = lambda b, j: B
    row = pl.BlockSpec((tr, D), lambda b, j: (b * nT + j, 0))
    vec = pl.BlockSpec((1, D), lambda b, j: (0, 0))
    small = pl.BlockSpec((tr, LANES), lambda b, j: (b * nT + j, 0))
    return pl.pallas_call(
        functools.partial(_post1_kernel, nC=nC, alpha=alpha, E=E),
        grid=(B, nT),
        in_specs=[row, row, vec,
                  _mod_spec(D, bi, 2), _mod_spec(D, ci, 2), _mod_spec(D, bi, 3), _mod_spec(D, ci, 3),
                  _mod_spec(D, bi, 4), _mod_spec(D, ci, 4), vec, vec,
                  pl.BlockSpec((D, E), lambda b, j: (0, 0)), pl.BlockSpec((1, E), lambda b, j: (0, 0))],
        out_specs=[row, row, small, small],
        out_shape=[jax.ShapeDtypeStruct((R, D), F32), jax.ShapeDtypeStruct((R, D), F32),
                   jax.ShapeDtypeStruct((R, LANES), jnp.int32), jax.ShapeDtypeStruct((R, LANES), F32)],
        compiler_params=_cp("arbitrary", "arbitrary"),
        name="post1",
    )(mo, xall, b_out.reshape(1, D), mod3, mod3, mod3, mod3, mod3, mod3,
      ln_g.reshape(1, D), ln_b.reshape(1, D), rw, rb.reshape(1, E))


def _moe_kernel(te_ref, tj0_ref, tok_ref, nact_ref, h_hbm, wg_ref, bg_ref, wu_ref, bu_ref, wd_ref, bd_ref,
                y_ref, buf, sem, *, tm):
    t = pl.program_id(0)
    nact = nact_ref[0]

    def row_copy(j0, i, slot):
        return pltpu.make_async_copy(h_hbm.at[pl.ds(tok_ref[j0 + i], 1), :], buf.at[slot, pl.ds(i, 1), :],
                                     sem.at[slot])

    def wait_tile(slot):
        pltpu.make_async_copy(h_hbm.at[pl.ds(0, tm), :], buf.at[slot], sem.at[slot]).wait()

    def gather(tile, slot):
        j0 = tj0_ref[tile]

        def issue(i, carry):
            row_copy(j0, i, slot).start()
            return carry

        lax.fori_loop(0, tm, issue, 0, unroll=8)

    @pl.when(t == 0)
    def _():
        gather(0, 0)

    @pl.when(t + 1 < nact)
    def _():
        gather(t + 1, (t + 1) % 2)

    @pl.when(t < nact)
    def _():
        slot = t % 2
        wait_tile(slot)
        xb = buf[slot].astype(BF16)
        gate = jnp.minimum(jnp.dot(xb, wg_ref[...], preferred_element_type=F32) + bg_ref[...], SWIGLU_LIMIT)
        up = jnp.clip(jnp.dot(xb, wu_ref[...], preferred_element_type=F32) + bu_ref[...], -SWIGLU_LIMIT, SWIGLU_LIMIT)
        act = ((up + 1.0) * gate * _sigmoid(SWIGLU_ALPHA * gate)).astype(BF16)
        y_ref[...] = jnp.dot(act, wd_ref[...], preferred_element_type=F32) + bd_ref[...]

    @pl.when(t >= nact)
    def _():
        y_ref[...] = jnp.zeros_like(y_ref)


def _moe(h2, te, tj0, tok, nact, wg, bg, wu, bu, wd, bd, l, tm, n_tiles):
    R, D = h2.shape
    F = wg.shape[2]
    em = lambda t, te, tj0, tok, nact: (te[t], 0, 0)
    eb = lambda t, te, tj0, tok, nact: (l, te[t], 0, 0)
    grid_spec = pltpu.PrefetchScalarGridSpec(
        num_scalar_prefetch=4,
        grid=(n_tiles,),
        in_specs=[pl.BlockSpec(memory_space=pl.ANY),
                  pl.BlockSpec((None, D, F), em), pl.BlockSpec((None, None, 1, F), eb),
                  pl.BlockSpec((None, D, F), em), pl.BlockSpec((None, None, 1, F), eb),
                  pl.BlockSpec((None, F, D), em), pl.BlockSpec((None, None, 1, D), eb)],
        out_specs=pl.BlockSpec((tm, D), lambda t, te, tj0, tok, nact: (t, 0)),
        scratch_shapes=[pltpu.VMEM((2, tm, D), F32), pltpu.SemaphoreType.DMA((2,))],
    )
    return pl.pallas_call(
        functools.partial(_moe_kernel, tm=tm),
        grid_spec=grid_spec,
        out_shape=jax.ShapeDtypeStruct((n_tiles * tm, D), F32),
        compiler_params=_cp("arbitrary"),
        name="moe",
    )(te, tj0, tok, nact, h2, wg, bg, wu, bu, wd, bd)


def _dispatch(idx, R, E, tm, n_tiles):
    e_flat = idx.reshape(-1)
    n_pairs = e_flat.shape[0]
    oh = e_flat[:, None] == jnp.arange(E, dtype=jnp.int32)[None, :]
    cs = jnp.cumsum(oh.astype(jnp.int32), axis=0)
    rank = jnp.sum(jnp.where(oh, cs, 0), axis=1) - 1
    counts = cs[-1]
    tiles_e = (counts + tm - 1) // tm
    tile_end = jnp.cumsum(tiles_e)
    tile_start = tile_end - tiles_e
    nact = tile_end[-1]
    cum_start = jnp.cumsum(counts) - counts
    dest = jnp.sum(jnp.where(oh, tile_start[None, :], 0), axis=1) * tm + rank
    tok = jnp.arange(n_pairs, dtype=jnp.int32) // TOP_K
    sorted_tok = jnp.concatenate([jnp.sort(e_flat * R + tok) % R, jnp.zeros((tm,), jnp.int32)])
    tt = jnp.minimum(jnp.arange(n_tiles, dtype=jnp.int32), nact - 1)
    te = jnp.sum((tt[:, None] >= tile_end[None, :]).astype(jnp.int32), axis=1)
    te = jnp.minimum(te, E - 1)
    tj0 = cum_start[te] + (tt - tile_start[te]) * tm
    return (te.astype(jnp.int32), tj0.astype(jnp.int32), sorted_tok.astype(jnp.int32),
            nact.reshape(1).astype(jnp.int32), dest.astype(jnp.int32))


def _combine_kernel(dest_ref, y_hbm, wt_ref, x1_ref, g2x, g2c, sh1x, sh1c, sc1x, sc1c, lg_ref, lb_ref,
                    *rest, tc, nT, j_off, nC, alpha, last):
    if last:
        out_ref, buf, sem = rest
    else:
        xo_ref, h_ref, buf, sem = rest
    nJ = nT - j_off
    step = pl.program_id(0) * nJ + pl.program_id(1)
    n_steps = pl.num_programs(0) * nJ
    j = pl.program_id(1) + j_off

    def row_copy(p0, i, k, slot):
        return pltpu.make_async_copy(y_hbm.at[pl.ds(dest_ref[p0 + i * TOP_K + k], 1), :],
                                     buf.at[slot, k, pl.ds(i, 1), :], sem.at[slot])

    def pair0(s):
        return ((s // nJ) * nT + (s % nJ) + j_off) * (tc * TOP_K)

    def wait_step(slot):
        for k in range(TOP_K):
            pltpu.make_async_copy(y_hbm.at[pl.ds(0, tc), :], buf.at[slot, k], sem.at[slot]).wait()

    def gather(s, slot):
        p0 = pair0(s)

        def issue(i, carry):
            for k in range(TOP_K):
                row_copy(p0, i, k, slot).start()
            return carry

        lax.fori_loop(0, tc, issue, 0, unroll=4)

    @pl.when(step == 0)
    def _():
        gather(0, 0)

    @pl.when(step + 1 < n_steps)
    def _():
        gather(step + 1, (step + 1) % 2)

    slot = step % 2
    wait_step(slot)
    wt = wt_ref[...]
    f = wt[:, 0:1] * buf[slot, 0]
    for k in range(1, TOP_K):
        f = f + wt[:, k:k + 1] * buf[slot, k]
    is_ctx = j < nC
    g2 = jnp.where(is_ctx, g2c[...], g2x[...])
    x2 = _ln(alpha * x1_ref[...] + g2 * f) * lg_ref[...] + lb_ref[...]
    if last:
        out_ref[...] = x2
    else:
        sh1 = jnp.where(is_ctx, sh1c[...], sh1x[...])
        sc1 = jnp.where(is_ctx, sc1c[...], sc1x[...])
        xo_ref[...] = x2
        h_ref[...] = (_ln(x2) * (1.0 + sc1) + sh1).astype(BF16)


def _combine(ys, dest, wts, x1, mod3, mod3_next, ln_g, ln_b, B, S, C, alpha, last):
    R, D = x1.shape
    Tb = C + S
    tc = 128
    nC = C // tc
    nT = Tb // tc
    j_off = nC if last else 0
    nJ = nT - j_off
    bi = lambda b, j, d: b
    ci = lambda b, j, d: B
    row = pl.BlockSpec((tc, D), lambda b, j, d: (b * nT + j + j_off, 0))
    vec = pl.BlockSpec((1, D), lambda b, j, d: (0, 0))
    if last:
        out_specs = pl.BlockSpec((tc, D), lambda b, j, d: (b * nJ + j, 0))
        out_shape = jax.ShapeDtypeStruct((B * S, D), F32)
    else:
        out_specs = [row, row]
        out_shape = [jax.ShapeDtypeStruct((R, D), F32), jax.ShapeDtypeStruct((R, D), BF16)]
    grid_spec = pltpu.PrefetchScalarGridSpec(
        num_scalar_prefetch=1,
        grid=(B, nJ),
        in_specs=[pl.BlockSpec(memory_space=pl.ANY),
                  pl.BlockSpec((tc, LANES), lambda b, j, d: (b * nT + j + j_off, 0)),
                  row,
                  _mod_spec(D, bi, 5), _mod_spec(D, ci, 5), _mod_spec(D, bi, 0), _mod_spec(D, ci, 0),
                  _mod_spec(D, bi, 1), _mod_spec(D, ci, 1), vec, vec],
        out_specs=out_specs,
        scratch_shapes=[pltpu.VMEM((2, TOP_K, tc, D), F32), pltpu.SemaphoreType.DMA((2,))],
    )
    return pl.pallas_call(
        functools.partial(_combine_kernel, tc=tc, nT=nT, j_off=j_off, nC=nC, alpha=alpha, last=last),
        grid_spec=grid_spec,
        out_shape=out_shape,
        compiler_params=_cp("arbitrary", "arbitrary"),
        name="combine",
    )(dest, ys, wts, x1, mod3, mod3, mod3_next, mod3_next, mod3_next, mod3_next,
      ln_g.reshape(1, D), ln_b.reshape(1, D))


def _rope_tables(S, C, head_dim):
    rows = S // GRID_W
    pos_row = jnp.repeat(jnp.arange(rows, dtype=jnp.int32), GRID_W).astype(F32)
    pos_col = jnp.tile(jnp.arange(GRID_W, dtype=jnp.int32), rows).astype(F32)
    n = head_dim // 4
    inv = ROPE_BASE ** (-jnp.arange(n, dtype=F32) / n)
    ang = jnp.concatenate([pos_row[:, None] * inv, pos_col[:, None] * inv], -1)
    cos, sin = jnp.cos(ang), jnp.sin(ang)
    reps = LANES // head_dim
    cos_f = jnp.tile(jnp.concatenate([cos, cos], -1), (1, reps))
    sin_f = jnp.tile(jnp.concatenate([-sin, sin], -1), (1, reps))
    cos_f = jnp.concatenate([jnp.ones((C, LANES), F32), cos_f], 0)
    sin_f = jnp.concatenate([jnp.zeros((C, LANES), F32), sin_f], 0)
    return cos_f, sin_f


def kernel(x, c, ctx, c_ctx, w_ada, b_ada, w_in, conv_w, conv_b, lru_wr, lru_br, lru_wi, lru_bi, lru_lambda, wa_sink, da_lq1, da_lk1, da_lq2, da_lk2, da_norm_g, w_br_a, w_br_b, w_br_c, w_out, b_out, ln1_g, ln1_b, router_w, router_b, exp_w_gate, exp_b_gate, exp_w_up, exp_b_up, exp_w_down, exp_b_down, ln2_g, ln2_b):
    B, S, D = x.shape
    C = ctx.shape[1]
    depth = w_ada.shape[0]
    Tb = C + S
    R = B * Tb
    d_in = w_in.shape[2]
    d_rnn = w_br_a.shape[1]
    wa_w = w_br_b.shape[1]
    da_w = w_br_c.shape[1]
    kv_w = (d_in - d_rnn - wa_w - 3 * da_w - 3 * D) // 2
    wa_heads = wa_w // WA_HEAD_DIM
    kvh_n = kv_w // WA_HEAD_DIM
    G = wa_heads // kvh_n
    da_heads = da_w // (2 * DA_HEAD_DIM)
    nrb = lru_wr.shape[2]
    E = router_w.shape[2]
    alpha = (2 * depth) ** 0.25
    assert B + 1 <= 8 and lru_wr.shape[3] == LANES and S % GRID_W == 0
    q_off = d_rnn
    k_off = q_off + wa_w
    v_off = k_off + kv_w
    qd_off = v_off + kv_w
    kd_off = qd_off + da_w
    vd_off = kd_off + da_w
    g_off = vd_off + da_w

    cos_wa, sin_wa = _rope_tables(S, C, WA_HEAD_DIM)
    cos_da, sin_da = _rope_tables(S, C, DA_HEAD_DIM)

    cc = jnp.concatenate([c, c_ctx[None, :], jnp.zeros((8 - B - 1, D), F32)], axis=0)
    b_ada3 = b_ada.reshape(depth, 1, 6 * D)
    mods = [_ada(cc, w_ada, b_ada3, l).reshape(8, 1, 6 * D) for l in range(depth)]

    xall, h = _ln0(x.reshape(B * S, D), ctx.reshape(B * C, D), mods[0], B, S, C)

    F = exp_w_gate.shape[3]
    wg3 = exp_w_gate.reshape(depth, E * D, F)
    wu3 = exp_w_up.reshape(depth, E * D, F)
    wd3 = exp_w_down.reshape(depth, E * F, D)
    bg_e = exp_b_gate.reshape(depth, E, 1, F)
    bu_e = exp_b_up.reshape(depth, E, 1, F)
    bd_e = exp_b_down.reshape(depth, E, 1, D)
    conv_b3 = conv_b.reshape(depth, 1, d_rnn)

    tm_moe = 256
    n_pairs = R * TOP_K
    n_tiles = n_pairs // tm_moe + E

    out = None
    for l in range(depth):
        last = l == depth - 1
        lam_init = 0.8 - 0.6 * math.exp(-0.3 * l)
        proj, (wg_b, wu_b, wd_b, wa_b, wb_b, wc_b) = _mm(
            h, w_in, l, 1088, 512,
            side=((wg3, E * D // 256), (wu3, E * D // 256), (wd3, E * F // 256),
                  (w_br_a, 16), (w_br_b, 16), (w_br_c, 16)))
        wg = jnp.concatenate([lru_wr[l, 0], lru_wi[l, 0], lru_wr[l, 1], lru_wi[l, 1]], axis=-1).astype(BF16)
        bg = jnp.stack([lru_br[l, 0].reshape(nrb, LANES), lru_bi[l, 0].reshape(nrb, LANES),
                        lru_br[l, 1].reshape(nrb, LANES), lru_bi[l, 1].reshape(nrb, LANES)], axis=1)
        bg = bg.reshape(nrb, 1, 4 * LANES)
        lam2 = jnp.stack([lru_lambda[l, 0].reshape(nrb, LANES), lru_lambda[l, 1].reshape(nrb, LANES)], axis=1)
        lam2 = lam2.reshape(nrb, 1, 2 * LANES)
        ya = _rglru(proj, conv_w, conv_b3, wg, bg, lam2, l, B, S, C, d_rnn)
        yb = _wattn(proj, wa_sink[l], cos_wa, sin_wa, B, S, C, q_off, k_off, v_off, kvh_n, G)
        lqk = jnp.stack([da_lq1[l], da_lk1[l], da_lq2[l], da_lk2[l]], axis=0)
        yc = _dattn(proj, cos_da, sin_da, da_norm_g[l].reshape(1, LANES), lqk, B, S, C,
                    qd_off, kd_off, vd_off, da_heads, lam_init)
        m = _merge(ya, yb, yc, wa_b, wb_b, wc_b, proj, g_off, D)
        mo, _ = _mm(m, w_out, l, 1088, 512)
        x1, h2, idx, wts = _post1(mo, xall, b_out[l], mods[l], ln1_g[l], ln1_b[l], router_w[l], router_b[l],
                                  B, S, C, alpha)
        te, tj0, tok, nact, dest = _dispatch(idx[:, :TOP_K], R, E, tm_moe, n_tiles)
        ys = _moe(h2, te, tj0, tok, nact, wg_b.reshape(E, D, F), bg_e, wu_b.reshape(E, D, F), bu_e,
                  wd_b.reshape(E, F, D), bd_e, l, tm_moe, n_tiles)
        nxt = mods[l + 1] if not last else mods[l]
        res = _combine(ys, dest, wts, x1, mods[l], nxt, ln2_g[l], ln2_b[l], B, S, C, alpha, last)
        if last:
            out = res.reshape(B, S, D)
        else:
            xall, h = res
    return out
```

```python
import functools
import math

import jax
import jax.numpy as jnp
from jax import lax
from jax.experimental import pallas as pl
from jax.experimental.pallas import tpu as pltpu

F32 = jnp.float32
BF16 = jnp.bfloat16

LN_EPS = 1e-5
GRID_W = 64
ROPE_BASE = 10000.0
LRU_C = 8.0
WA_HEAD_DIM = 128
DA_HEAD_DIM = 64
WINDOW = 128
TOP_K = 4
SWIGLU_LIMIT = 7.0
SWIGLU_ALPHA = 1.702
LANES = 128
SUBLANES = 8
NEG = -1e30
LOG2E = 1.4426950408889634
DA_CHUNKS = 17
ONES_ROWS = 16
VMEM_LIMIT = 56 * 1024 * 1024


def _cp(*sem):
    return pltpu.CompilerParams(dimension_semantics=sem, vmem_limit_bytes=VMEM_LIMIT)


def _sigmoid(v):
    return 1.0 / (1.0 + jnp.exp(-v))


def _ln(v):
    mu = jnp.mean(v, axis=-1, keepdims=True)
    d = v - mu
    var = jnp.mean(d * d, axis=-1, keepdims=True)
    return d * lax.rsqrt(var + LN_EPS)


def _ada_kernel(c_ref, w_ref, b_ref, o_ref):
    c = c_ref[...]
    s = (c * _sigmoid(c)).astype(BF16)
    o_ref[...] = jnp.dot(s, w_ref[...].astype(BF16), preferred_element_type=F32) + b_ref[...]


def _ada(cc, w, b, l):
    _, D, N = w.shape
    tn = 512
    return pl.pallas_call(
        _ada_kernel,
        grid=(N // tn,),
        in_specs=[pl.BlockSpec((8, D), lambda j: (0, 0)),
                  pl.BlockSpec((None, D, tn), lambda j: (l, 0, j)),
                  pl.BlockSpec((None, 1, tn), lambda j: (l, 0, j))],
        out_specs=pl.BlockSpec((8, tn), lambda j: (0, j)),
        out_shape=jax.ShapeDtypeStruct((8, N), F32),
        compiler_params=_cp("arbitrary"),
        name="ada",
    )(cc, w, b)


def _mod_spec(D, row_fn, chunk):
    return pl.BlockSpec((None, 1, D), lambda *a: (row_fn(*a), 0, chunk))


def _ln0_kernel(x_ref, c_ref, shx, scx, shc, scc, xo_ref, h_ref, *, nC):
    is_ctx = pl.program_id(1) < nC
    v = jnp.where(is_ctx, c_ref[...], x_ref[...])
    sh = jnp.where(is_ctx, shc[...], shx[...])
    sc = jnp.where(is_ctx, scc[...], scx[...])
    xo_ref[...] = v
    h_ref[...] = (_ln(v) * (1.0 + sc) + sh).astype(BF16)


def _ln0(x2, ctx2, mod3, B, S, C):
    D = x2.shape[1]
    Tb = C + S
    tr = 256
    nC = C // tr
    nS = S // tr
    nT = Tb // tr
    bi = lambda b, j: b
    ci = lambda b, j: B
    return pl.pallas_call(
        functools.partial(_ln0_kernel, nC=nC),
        grid=(B, nT),
        in_specs=[pl.BlockSpec((tr, D), lambda b, j: (b * nS + jnp.maximum(j - nC, 0), 0)),
                  pl.BlockSpec((tr, D), lambda b, j: (b * nC + jnp.minimum(j, nC - 1), 0)),
                  _mod_spec(D, bi, 0), _mod_spec(D, bi, 1), _mod_spec(D, ci, 0), _mod_spec(D, ci, 1)],
        out_specs=[pl.BlockSpec((tr, D), lambda b, j: (b * nT + j, 0)),
                   pl.BlockSpec((tr, D), lambda b, j: (b * nT + j, 0))],
        out_shape=[jax.ShapeDtypeStruct((B * Tb, D), F32), jax.ShapeDtypeStruct((B * Tb, D), BF16)],
        compiler_params=_cp("arbitrary", "arbitrary"),
        name="ln0",
    )(x2, ctx2, mod3, mod3, mod3, mod3)


def _mm_kernel(a_ref, w_ref, *rest, n_side):
    o_ref = rest[n_side]
    o_ref[...] = jnp.dot(a_ref[...], w_ref[...].astype(BF16), preferred_element_type=F32).astype(o_ref.dtype)
    for src, dst in zip(rest[:n_side], rest[n_side + 1:]):
        dst[...] = src[...].astype(BF16)


def _mm(a, w, l, tm, tn, side=(), a_buffers=2):
    M, K = a.shape
    N = w.shape[2]
    nj = N // tn
    in_specs = [pl.BlockSpec((tm, K), lambda i, j: (i, 0), pipeline_mode=pl.Buffered(a_buffers)),
                pl.BlockSpec((None, K, tn), lambda i, j: (l, 0, j))]
    out_specs = [pl.BlockSpec((tm, tn), lambda i, j: (i, j))]
    out_shape = [jax.ShapeDtypeStruct((M, N), BF16)]
    for arr, br in side:
        _, rows, cols = arr.shape
        last = rows // br - 1
        assert rows % br == 0 and last < (M // tm) * nj
        in_specs.append(pl.BlockSpec((None, br, cols), lambda i, j, last=last: (l, jnp.minimum(i * nj + j, last), 0)))
        out_specs.append(pl.BlockSpec((br, cols), lambda i, j, last=last: (jnp.minimum(i * nj + j, last), 0)))
        out_shape.append(jax.ShapeDtypeStruct((rows, cols), BF16))
    res = pl.pallas_call(
        functools.partial(_mm_kernel, n_side=len(side)),
        grid=(M // tm, nj),
        in_specs=in_specs,
        out_specs=out_specs,
        out_shape=out_shape,
        compiler_params=_cp("arbitrary", "arbitrary"),
        name="mm",
    )(a, w, *[arr for arr, _ in side])
    return res[0], res[1:]


def _rglru_kernel(u_ref, cw_ref, cb_ref, wg_ref, bg_ref, lam_ref, y_ref, h_scr, a_scr, o_scr, *, C, S):
    T = C + S
    nv = T // SUBLANES
    u = u_ref[...].astype(F32)
    cw = cw_ref[...]
    cb = cb_ref[...]
    sub8 = lax.broadcasted_iota(jnp.int32, (SUBLANES, LANES), 0)

    def conv_seq(us):
        n = us.shape[0]
        t_m1 = pltpu.roll(us, 1, 0)
        t_m1 = jnp.concatenate([jnp.where(sub8 < 1, 0.0, t_m1[0:SUBLANES]), t_m1[SUBLANES:]], axis=0)
        t_p1 = pltpu.roll(us, n - 1, 0)
        t_p1 = jnp.concatenate([t_p1[:n - SUBLANES], jnp.where(sub8 >= SUBLANES - 1, 0.0, t_p1[n - SUBLANES:])],
                               axis=0)
        t_p2 = pltpu.roll(us, n - 2, 0)
        t_p2 = jnp.concatenate([t_p2[:n - SUBLANES], jnp.where(sub8 >= SUBLANES - 2, 0.0, t_p2[n - SUBLANES:])],
                               axis=0)
        return cb + cw[0:1] * t_m1 + cw[1:2] * us + cw[2:3] * t_p1 + cw[3:4] * t_p2

    v = jnp.concatenate([conv_seq(u[0:C]), conv_seq(u[C:T])], axis=0)
    g = jnp.dot(v.astype(BF16), wg_ref[...], preferred_element_type=F32) + bg_ref[...]
    lam = lam_ref[...]
    nl = -lam
    sp = jnp.maximum(nl, 0.0) + jnp.log1p(jnp.exp(-jnp.abs(nl)))

    for d in range(2):
        r = 0.5 * jnp.tanh(0.5 * g[:, (2 * d) * LANES:(2 * d + 1) * LANES]) + 0.5
        i = 0.5 * jnp.tanh(0.5 * g[:, (2 * d + 1) * LANES:(2 * d + 2) * LANES]) + 0.5
        a = jnp.exp((-LRU_C) * sp[:, d * LANES:(d + 1) * LANES] * r)
        a_scr[d] = a
        h_scr[d] = jnp.sqrt((1.0 - a) * (1.0 + a)) * (i * v)

    nvc = C // SUBLANES

    def scan_group(d, k):
        rows = pl.ds(pl.multiple_of(k * SUBLANES, SUBLANES), SUBLANES)
        a8 = a_scr[d, rows, :]
        b8 = h_scr[d, rows, :]
        for s in (1, 2, 4):
            if d == 0:
                ok = sub8 >= s
                shift = s
            else:
                ok = sub8 < SUBLANES - s
                shift = SUBLANES - s
            a_sh = jnp.where(ok, pltpu.roll(a8, shift, 0), 1.0)
            b_sh = jnp.where(ok, pltpu.roll(b8, shift, 0), 0.0)
            b8 = a8 * b_sh + b8
            a8 = a8 * a_sh
        return b8, a8

    def rev_group(k):
        return jnp.where(k < nvc, nvc - 1 - k, nv - 1 - (k - nvc))

    def step(k, carry):
        cf, cr, hf, pf, hr, pr = carry
        of = hf + pf * cf
        o_scr[0, pl.ds(pl.multiple_of(k * SUBLANES, SUBLANES), SUBLANES), :] = of
        cf = jnp.broadcast_to(of[SUBLANES - 1:SUBLANES, :], (SUBLANES, LANES))
        orv = hr + pr * cr
        o_scr[1, pl.ds(pl.multiple_of(rev_group(k) * SUBLANES, SUBLANES), SUBLANES), :] = orv
        cr = jnp.broadcast_to(orv[0:1, :], (SUBLANES, LANES))
        kn = jnp.minimum(k + 1, nv - 1)
        hf, pf = scan_group(0, kn)
        hr, pr = scan_group(1, rev_group(kn))
        return cf, cr, hf, pf, hr, pr

    z = jnp.zeros((SUBLANES, LANES), F32)
    lax.fori_loop(0, nv, step, (z, z) + scan_group(0, 0) + scan_group(1, rev_group(0)), unroll=8)
    y_ref[...] = (o_scr[0] + o_scr[1]).astype(BF16)


def _rglru(proj, conv_w, conv_b, wg, bg, lam2, l, B, S, C, d_rnn):
    Tb = C + S
    nb = d_rnn // LANES
    return pl.pallas_call(
        functools.partial(_rglru_kernel, C=C, S=S),
        grid=(B, nb),
        in_specs=[pl.BlockSpec((Tb, LANES), lambda b, n: (b, n)),
                  pl.BlockSpec((None, 4, LANES), lambda b, n: (l, 0, n)),
                  pl.BlockSpec((None, 1, LANES), lambda b, n: (l, 0, n)),
                  pl.BlockSpec((None, LANES, 4 * LANES), lambda b, n: (n, 0, 0)),
                  pl.BlockSpec((None, 1, 4 * LANES), lambda b, n: (n, 0, 0)),
                  pl.BlockSpec((None, 1, 2 * LANES), lambda b, n: (n, 0, 0))],
        out_specs=pl.BlockSpec((Tb, LANES), lambda b, n: (b, n)),
        out_shape=jax.ShapeDtypeStruct((B * Tb, d_rnn), BF16),
        scratch_shapes=[pltpu.VMEM((2, Tb, LANES), F32), pltpu.VMEM((2, Tb, LANES), F32),
                        pltpu.VMEM((2, Tb, LANES), F32)],
        compiler_params=_cp("arbitrary", "arbitrary"),
        name="rglru",
    )(proj, conv_w, conv_b, wg, bg, lam2)


def _wattn_kernel(sink_ref, q_ref, k_ref, v_ref, cos_ref, sin_ref, o_ref, kr_scr, vt_scr, *, C, Tb, G, HP):
    hg = pl.program_id(1)
    j = pl.program_id(2)
    W = WINDOW
    nblk = Tb // W
    ncb = C // W
    qw = G * WA_HEAD_DIM
    scale = WA_HEAD_DIM ** -0.5
    nt = (((1,), (1,)), ((), ()))

    @pl.when(j == 0)
    def _():
        zk = jnp.zeros((W, WA_HEAD_DIM), BF16)
        for hp in range(HP):
            kf = k_ref[:, hp * WA_HEAD_DIM:(hp + 1) * WA_HEAD_DIM].astype(F32)
            kr = kf * cos_ref[...] + pltpu.roll(kf, WA_HEAD_DIM // 2, 1) * sin_ref[...]
            kr_scr[hp, 0:W, :] = zk
            kr_scr[hp, W:W + Tb, :] = kr.astype(BF16)
            kr_scr[hp, W + Tb:2 * W + Tb, :] = zk
            vt = v_ref[:, hp * WA_HEAD_DIM:(hp + 1) * WA_HEAD_DIM].astype(F32).T.astype(BF16)
            vt_scr[hp, 0] = zk
            vt_scr[hp, nblk + 1] = zk
            for bk in range(nblk):
                vt_scr[hp, bk + 1] = vt[:, bk * W:(bk + 1) * W]

    r0 = pl.multiple_of(j * W, W)
    cs = cos_ref[pl.ds(r0, W), :]
    sn = sin_ref[pl.ds(r0, W), :]
    ki = lax.broadcasted_iota(jnp.int32, (3 * W, W), 0)
    qi = lax.broadcasted_iota(jnp.int32, (3 * W, W), 1)
    lo = C - (j - 1) * W
    hi = Tb - (j - 1) * W
    valid = (ki >= qi) & (ki <= qi + 2 * W) & (ki >= lo) & (ki < hi) & (j >= ncb)
    lane = lax.broadcasted_iota(jnp.int32, (1, G * W), 1)
    for hp in range(HP):
        qs = []
        for g in range(G):
            qg = q_ref[:, hp * qw + g * WA_HEAD_DIM:hp * qw + (g + 1) * WA_HEAD_DIM].astype(F32)
            qs.append((qg * cs + pltpu.roll(qg, WA_HEAD_DIM // 2, 1) * sn) * scale)
        qq = jnp.concatenate(qs, axis=0).astype(BF16)
        sl = lax.dot_general(kr_scr[hp, pl.ds(r0, 3 * W), :], qq, nt, preferred_element_type=F32)
        sx = lax.dot_general(kr_scr[hp, W:W + C, :], qq, nt, preferred_element_type=F32)
        sl = jnp.concatenate([jnp.where(valid, sl[:, g * W:(g + 1) * W], NEG) for g in range(G)], axis=1)
        sink = jnp.zeros((1, G * W), F32)
        for g in range(G):
            sink = jnp.where(lane // W == g, sink_ref[(hg * HP + hp) * G + g], sink)
        m = jnp.maximum(jnp.maximum(jnp.max(sl, axis=0, keepdims=True), jnp.max(sx, axis=0, keepdims=True)), sink)
        el = jnp.exp(sl - m)
        ex = jnp.exp(sx - m)
        l = jnp.sum(el, axis=0, keepdims=True) + jnp.sum(ex, axis=0, keepdims=True) + jnp.exp(sink - m)
        elb = el.astype(BF16)
        exb = ex.astype(BF16)
        ot = jnp.dot(vt_scr[hp, j], elb[0:W], preferred_element_type=F32)
        ot = ot + jnp.dot(vt_scr[hp, j + 1], elb[W:2 * W], preferred_element_type=F32)
        ot = ot + jnp.dot(vt_scr[hp, j + 2], elb[2 * W:3 * W], preferred_element_type=F32)
        for cb in range(ncb):
            ot = ot + jnp.dot(vt_scr[hp, 1 + cb], exb[cb * W:(cb + 1) * W], preferred_element_type=F32)
        ot = ot * (1.0 / l)
        for g in range(G):
            o_ref[:, hp * qw + g * WA_HEAD_DIM:hp * qw + (g + 1) * WA_HEAD_DIM] = (
                ot[:, g * W:(g + 1) * W].T.astype(BF16))


def _wattn(proj, sink, cos, sin, B, S, C, q_off, k_off, v_off, kvh_n, G):
    Tb = C + S
    nblk = Tb // WINDOW
    HP = 2 if kvh_n % 2 == 0 else 1
    qw = HP * G * WA_HEAD_DIM
    kw = HP * WA_HEAD_DIM
    assert C % WINDOW == 0 and WINDOW == WA_HEAD_DIM == LANES
    assert q_off % qw == 0 and k_off % kw == 0 and v_off % kw == 0
    qc0, kc0, vc0 = q_off // qw, k_off // kw, v_off // kw

    def slab(col0):
        return pl.BlockSpec((Tb, kw), lambda b, h, j: (b, col0 + h))

    full = pl.BlockSpec((Tb, WA_HEAD_DIM), lambda b, h, j: (0, 0))
    return pl.pallas_call(
        functools.partial(_wattn_kernel, C=C, Tb=Tb, G=G, HP=HP),
        grid=(B, kvh_n // HP, nblk),
        in_specs=[pl.BlockSpec(memory_space=pltpu.SMEM),
                  pl.BlockSpec((WINDOW, qw), lambda b, h, j: (b * nblk + j, qc0 + h)),
                  slab(kc0), slab(vc0), full, full],
        out_specs=pl.BlockSpec((WINDOW, qw), lambda b, h, j: (b * nblk + j, h)),
        out_shape=jax.ShapeDtypeStruct((B * Tb, (kvh_n // HP) * qw), BF16),
        scratch_shapes=[pltpu.VMEM((HP, Tb + 2 * WINDOW, WA_HEAD_DIM), BF16),
                        pltpu.VMEM((HP, nblk + 2, WA_HEAD_DIM, WINDOW), BF16)],
        compiler_params=_cp("arbitrary", "arbitrary", "arbitrary"),
        name="wattn",
    )(sink, proj, proj, proj, cos, sin)


def _dattn_kernel(q_ref, k_ref, v_ref, cos_ref, sin_ref, g_ref, lqk_ref, o_ref, kr_scr, vt_scr, s0_scr, s1_scr,
                  *, C, S, tq, lam_init):
    Tb = C + S
    half = DA_HEAD_DIM // 2
    lane = lax.broadcasted_iota(jnp.int32, (1, LANES), 1)
    lo = (lane % DA_HEAD_DIM) < half
    nt = (((1,), (1,)), ((), ()))

    def rope(xv, cs, sn):
        rot = jnp.where(lo, pltpu.roll(xv, LANES - half, 1), pltpu.roll(xv, half, 1))
        return xv * cs + rot * sn

    kr_scr[...] = rope(k_ref[...].astype(F32), cos_ref[...], sin_ref[...]).astype(BF16)
    vt_scr[0:LANES, :] = v_ref[...].astype(F32).T.astype(BF16)
    vt_scr[LANES:LANES + ONES_ROWS, :] = jnp.ones((ONES_ROWS, Tb), BF16)
    lq = lqk_ref[...]
    lam = (jnp.exp(jnp.sum(lq[0:1] * lq[1:2], axis=1, keepdims=True))
           - jnp.exp(jnp.sum(lq[2:3] * lq[3:4], axis=1, keepdims=True)) + lam_init)
    gain = g_ref[...] * (1.0 - lam_init)

    def make_qq(r0):
        qv = rope(q_ref[pl.ds(r0, tq), :].astype(F32), cos_ref[pl.ds(r0, tq), :], sin_ref[pl.ds(r0, tq), :]) * (
            DA_HEAD_DIM ** -0.5 * LOG2E)
        return jnp.concatenate([jnp.where(lane < DA_HEAD_DIM, qv, 0.0), jnp.where(lane >= DA_HEAD_DIM, qv, 0.0)],
                               axis=0).astype(BF16)

    def write_out(r0, otl):
        ot = otl[0:LANES]
        l = otl[LANES:LANES + 1]
        o = (ot[:, 0:tq] * (1.0 / l[:, 0:tq]) - ot[:, tq:2 * tq] * (lam / l[:, tq:2 * tq])).T
        o = o * lax.rsqrt(jnp.mean(o * o, axis=1, keepdims=True) + LN_EPS) * gain
        o_ref[pl.ds(r0, tq), :] = o.astype(BF16)

    def scores(r0, s_scr, nk):
        s = lax.dot_general(kr_scr[0:nk, :], make_qq(r0), nt, preferred_element_type=F32)
        s_scr[0:nk, :] = s
        return jnp.max(s, axis=0, keepdims=True)

    def finish(r0, s_scr, m, nk):
        e = jnp.exp2(s_scr[0:nk, :] - m)
        write_out(r0, jnp.dot(vt_scr[:, 0:nk], e.astype(BF16), preferred_element_type=F32))

    for i in range(C // tq):
        finish(i * tq, s0_scr, scores(i * tq, s0_scr, C), C)

    n_lat = S // tq
    m0 = scores(C, s0_scr, Tb)

    units = Tb // tq
    cuts = [round(units * c / DA_CHUNKS) * tq for c in range(DA_CHUNKS + 1)]

    def fused(r_s, s_w, r_f, s_r, m_f):
        qq = make_qq(r_s)
        m_s = ot = None
        for c in range(DA_CHUNKS):
            k0, k1 = cuts[c], cuts[c + 1]
            s = lax.dot_general(kr_scr[k0:k1, :], qq, nt, preferred_element_type=F32)
            s_w[k0:k1, :] = s
            cm = jnp.max(s, axis=0, keepdims=True)
            m_s = cm if c == 0 else jnp.maximum(m_s, cm)
            e = jnp.exp2(s_r[k0:k1, :] - m_f)
            co = jnp.dot(vt_scr[:, k0:k1], e.astype(BF16), preferred_element_type=F32)
            ot = co if c == 0 else ot + co
        write_out(r_f, ot)
        return m_s

    def pair(p, m0):
        r = pl.multiple_of(C + p * (2 * tq), tq)
        m1 = fused(r + tq, s1_scr, r, s0_scr, m0)
        return fused(r + 2 * tq, s0_scr, r + tq, s1_scr, m1)

    m0 = lax.fori_loop(0, n_lat // 2 - 1, pair, m0)
    r = C + (n_lat - 2) * tq
    m1 = fused(r + tq, s1_scr, r, s0_scr, m0)
    finish(r + tq, s1_scr, m1, Tb)


def _dattn(proj, cos, sin, g, lqk, B, S, C, q_off, k_off, v_off, heads, lam_init):
    Tb = C + S
    tq = 256
    assert C % tq == 0 and (S // tq) % 2 == 0 and S // tq >= 2
    qc0, kc0, vc0 = q_off // LANES, k_off // LANES, v_off // LANES
    full = pl.BlockSpec((Tb, LANES), lambda b, h: (0, 0))

    def slab(col0):
        return pl.BlockSpec((Tb, LANES), lambda b, h: (b, col0 + h))

    return pl.pallas_call(
        functools.partial(_dattn_kernel, C=C, S=S, tq=tq, lam_init=lam_init),
        grid=(B, heads),
        in_specs=[slab(qc0), slab(kc0), slab(vc0), full, full,
                  pl.BlockSpec((1, LANES), lambda b, h: (0, 0)),
                  pl.BlockSpec((4, DA_HEAD_DIM), lambda b, h: (0, 0))],
        out_specs=pl.BlockSpec((Tb, LANES), lambda b, h: (b, h)),
        out_shape=jax.ShapeDtypeStruct((B * Tb, heads * LANES), BF16),
        scratch_shapes=[pltpu.VMEM((Tb, LANES), BF16), pltpu.VMEM((LANES + ONES_ROWS, Tb), BF16),
                        pltpu.VMEM((Tb, 2 * tq), F32), pltpu.VMEM((Tb, 2 * tq), F32)],
        compiler_params=_cp("arbitrary", "arbitrary"),
        name="dattn",
    )(proj, proj, proj, cos, sin, g, lqk)


def _merge_kernel(ya_ref, yb_ref, yc_ref, wa_ref, wb_ref, wc_ref, ga_ref, gb_ref, gc_ref, o_ref):
    def br(y_ref, w_ref, g_ref):
        return _sigmoid(g_ref[...].astype(F32)) * jnp.dot(y_ref[...], w_ref[...], preferred_element_type=F32)
    o_ref[...] = (br(ya_ref, wa_ref, ga_ref) + br(yb_ref, wb_ref, gb_ref) + br(yc_ref, wc_ref, gc_ref)).astype(BF16)


def _merge(ya, yb, yc, wa, wb, wc, proj, g_off, D):
    R = ya.shape[0]
    tm, tn = 544, 512
    g0 = g_off // tn
    nd = D // tn

    def yspec(y):
        return pl.BlockSpec((tm, y.shape[1]), lambda i, j: (i, 0))

    def wspec(w):
        return pl.BlockSpec((w.shape[0], tn), lambda i, j: (0, j))

    def gspec(k):
        return pl.BlockSpec((tm, tn), lambda i, j: (i, g0 + k * nd + j))

    return pl.pallas_call(
        _merge_kernel,
        grid=(R // tm, nd),
        in_specs=[yspec(ya), yspec(yb), yspec(yc), wspec(wa), wspec(wb), wspec(wc), gspec(0), gspec(1), gspec(2)],
        out_specs=pl.BlockSpec((tm, tn), lambda i, j: (i, j)),
        out_shape=jax.ShapeDtypeStruct((R, D), BF16),
        compiler_params=_cp("arbitrary", "arbitrary"),
        name="merge",
    )(ya, yb, yc, wa, wb, wc, proj, proj, proj)


def _post1_kernel(mo_ref, x_ref, bo_ref, g1x, g1c, sh2x, sh2c, sc2x, sc2c, lg_ref, lb_ref, rw_ref, rb_ref,
                  x1_ref, h2_ref, idx_ref, wt_ref, *, nC, alpha, E):
    is_ctx = pl.program_id(1) < nC
    g1 = jnp.where(is_ctx, g1c[...], g1x[...])
    sh2 = jnp.where(is_ctx, sh2c[...], sh2x[...])
    sc2 = jnp.where(is_ctx, sc2c[...], sc2x[...])
    y = alpha * x_ref[...] + g1 * (mo_ref[...].astype(F32) + bo_ref[...])
    x1 = _ln(y) * lg_ref[...] + lb_ref[...]
    x1_ref[...] = x1
    h2 = _ln(x1) * (1.0 + sc2) + sh2
    h2_ref[...] = h2
    logits = jnp.dot(h2, rw_ref[...], preferred_element_type=F32, precision=lax.Precision.HIGHEST) + rb_ref[...]
    col = lax.broadcasted_iota(jnp.int32, logits.shape, 1)
    lane = lax.broadcasted_iota(jnp.int32, (logits.shape[0], LANES), 1)
    idx_out = jnp.zeros((logits.shape[0], LANES), jnp.int32)
    val_out = jnp.zeros((logits.shape[0], LANES), F32)
    lg = logits
    v0 = None
    den = 0.0
    for k in range(TOP_K):
        m = jnp.max(lg, axis=1, keepdims=True)
        am = jnp.min(jnp.where(lg == m, col, E), axis=1, keepdims=True)
        lg = jnp.where(col == am, -jnp.inf, lg)
        if k == 0:
            v0 = m
        e = jnp.exp(m - v0)
        den = den + e
        idx_out = jnp.where(lane == k, am, idx_out)
        val_out = jnp.where(lane == k, e, val_out)
    idx_ref[...] = idx_out
    wt_ref[...] = val_out / den


def _post1(mo, xall, b_out, mod3, ln_g, ln_b, rw, rb, B, S, C, alpha):
    R, D = xall.shape
    E = rw.shape[1]
    Tb = C + S
    tr = 256
    nC = C // tr
    nT = Tb // tr
    bi = lambda b, j: b
    ci = lambda b, j: B
    row = pl.BlockSpec((tr, D), lambda b, j: (b * nT + j, 0))
    vec = pl.BlockSpec((1, D), lambda b, j: (0, 0))
    small = pl.BlockSpec((tr, LANES), lambda b, j: (b * nT + j, 0))
    return pl.pallas_call(
        functools.partial(_post1_kernel, nC=nC, alpha=alpha, E=E),
        grid=(B, nT),
        in_specs=[row, row, vec,
                  _mod_spec(D, bi, 2), _mod_spec(D, ci, 2), _mod_spec(D, bi, 3), _mod_spec(D, ci, 3),
                  _mod_spec(D, bi, 4), _mod_spec(D, ci, 4), vec, vec,
                  pl.BlockSpec((D, E), lambda b, j: (0, 0)), pl.BlockSpec((1, E), lambda b, j: (0, 0))],
        out_specs=[row, row, small, small],
        out_shape=[jax.ShapeDtypeStruct((R, D), F32), jax.ShapeDtypeStruct((R, D), F32),
                   jax.ShapeDtypeStruct((R, LANES), jnp.int32), jax.ShapeDtypeStruct((R, LANES), F32)],
        compiler_params=_cp("arbitrary", "arbitrary"),
        name="post1",
    )(mo, xall, b_out.reshape(1, D), mod3, mod3, mod3, mod3, mod3, mod3,
      ln_g.reshape(1, D), ln_b.reshape(1, D), rw, rb.reshape(1, E))


def _moe_kernel(te_ref, tj0_ref, tok_ref, nact_ref, h_hbm, wg_ref, bg_ref, wu_ref, bu_ref, wd_ref, bd_ref,
                y_ref, buf, sem, *, tm):
    t = pl.program_id(0)
    nact = nact_ref[0]

    def row_copy(j0, i, slot):
        return pltpu.make_async_copy(h_hbm.at[pl.ds(tok_ref[j0 + i], 1), :], buf.at[slot, pl.ds(i, 1), :],
                                     sem.at[slot])

    def wait_tile(slot):
        pltpu.make_async_copy(h_hbm.at[pl.ds(0, tm), :], buf.at[slot], sem.at[slot]).wait()

    def gather(tile, slot):
        j0 = tj0_ref[tile]

        def issue(i, carry):
            row_copy(j0, i, slot).start()
            return carry

        lax.fori_loop(0, tm, issue, 0, unroll=8)

    @pl.when(t == 0)
    def _():
        gather(0, 0)

    @pl.when(t + 1 < nact)
    def _():
        gather(t + 1, (t + 1) % 2)

    @pl.when(t < nact)
    def _():
        slot = t % 2
        wait_tile(slot)
        xb = buf[slot].astype(BF16)
        gate = jnp.minimum(jnp.dot(xb, wg_ref[...], preferred_element_type=F32) + bg_ref[...], SWIGLU_LIMIT)
        up = jnp.clip(jnp.dot(xb, wu_ref[...], preferred_element_type=F32) + bu_ref[...], -SWIGLU_LIMIT, SWIGLU_LIMIT)
        act = ((up + 1.0) * gate * _sigmoid(SWIGLU_ALPHA * gate)).astype(BF16)
        y_ref[...] = jnp.dot(act, wd_ref[...], preferred_element_type=F32) + bd_ref[...]

    @pl.when(t >= nact)
    def _():
        y_ref[...] = jnp.zeros_like(y_ref)


def _moe(h2, te, tj0, tok, nact, wg, bg, wu, bu, wd, bd, l, tm, n_tiles):
    R, D = h2.shape
    F = wg.shape[2]
    em = lambda t, te, tj0, tok, nact: (te[t], 0, 0)
    eb = lambda t, te, tj0, tok, nact: (l, te[t], 0, 0)
    grid_spec = pltpu.PrefetchScalarGridSpec(
        num_scalar_prefetch=4,
        grid=(n_tiles,),
        in_specs=[pl.BlockSpec(memory_space=pl.ANY),
                  pl.BlockSpec((None, D, F), em), pl.BlockSpec((None, None, 1, F), eb),
                  pl.BlockSpec((None, D, F), em), pl.BlockSpec((None, None, 1, F), eb),
                  pl.BlockSpec((None, F, D), em), pl.BlockSpec((None, None, 1, D), eb)],
        out_specs=pl.BlockSpec((tm, D), lambda t, te, tj0, tok, nact: (t, 0)),
        scratch_shapes=[pltpu.VMEM((2, tm, D), F32), pltpu.SemaphoreType.DMA((2,))],
    )
    return pl.pallas_call(
        functools.partial(_moe_kernel, tm=tm),
        grid_spec=grid_spec,
        out_shape=jax.ShapeDtypeStruct((n_tiles * tm, D), F32),
        compiler_params=_cp("arbitrary"),
        name="moe",
    )(te, tj0, tok, nact, h2, wg, bg, wu, bu, wd, bd)


def _dispatch(idx, R, E, tm, n_tiles):
    e_flat = idx.reshape(-1)
    n_pairs = e_flat.shape[0]
    oh = e_flat[:, None] == jnp.arange(E, dtype=jnp.int32)[None, :]
    cs = jnp.cumsum(oh.astype(jnp.int32), axis=0)
    rank = jnp.sum(jnp.where(oh, cs, 0), axis=1) - 1
    counts = cs[-1]
    tiles_e = (counts + tm - 1) // tm
    tile_end = jnp.cumsum(tiles_e)
    tile_start = tile_end - tiles_e
    nact = tile_end[-1]
    cum_start = jnp.cumsum(counts) - counts
    dest = jnp.sum(jnp.where(oh, tile_start[None, :], 0), axis=1) * tm + rank
    tok = jnp.arange(n_pairs, dtype=jnp.int32) // TOP_K
    sorted_tok = jnp.concatenate([jnp.sort(e_flat * R + tok) % R, jnp.zeros((tm,), jnp.int32)])
    tt = jnp.minimum(jnp.arange(n_tiles, dtype=jnp.int32), nact - 1)
    te = jnp.sum((tt[:, None] >= tile_end[None, :]).astype(jnp.int32), axis=1)
    te = jnp.minimum(te, E - 1)
    tj0 = cum_start[te] + (tt - tile_start[te]) * tm
    return (te.astype(jnp.int32), tj0.astype(jnp.int32), sorted_tok.astype(jnp.int32),
            nact.reshape(1).astype(jnp.int32), dest.astype(jnp.int32))


def _combine_kernel(dest_ref, y_hbm, wt_ref, x1_ref, g2x, g2c, sh1x, sh1c, sc1x, sc1c, lg_ref, lb_ref,
                    *rest, tc, nT, j_off, nC, alpha, last):
    if last:
        out_ref, buf, sem = rest
    else:
        xo_ref, h_ref, buf, sem = rest
    nJ = nT - j_off
    step = pl.program_id(0) * nJ + pl.program_id(1)
    n_steps = pl.num_programs(0) * nJ
    j = pl.program_id(1) + j_off

    def row_copy(p0, i, k, slot):
        return pltpu.make_async_copy(y_hbm.at[pl.ds(dest_ref[p0 + i * TOP_K + k], 1), :],
                                     buf.at[slot, k, pl.ds(i, 1), :], sem.at[slot])

    def pair0(s):
        return ((s // nJ) * nT + (s % nJ) + j_off) * (tc * TOP_K)

    def wait_step(slot):
        for k in range(TOP_K):
            pltpu.make_async_copy(y_hbm.at[pl.ds(0, tc), :], buf.at[slot, k], sem.at[slot]).wait()

    def gather(s, slot):
        p0 = pair0(s)

        def issue(i, carry):
            for k in range(TOP_K):
                row_copy(p0, i, k, slot).start()
            return carry

        lax.fori_loop(0, tc, issue, 0, unroll=4)

    @pl.when(step == 0)
    def _():
        gather(0, 0)

    @pl.when(step + 1 < n_steps)
    def _():
        gather(step + 1, (step + 1) % 2)

    slot = step % 2
    wait_step(slot)
    wt = wt_ref[...]
    f = wt[:, 0:1] * buf[slot, 0]
    for k in range(1, TOP_K):
        f = f + wt[:, k:k + 1] * buf[slot, k]
    is_ctx = j < nC
    g2 = jnp.where(is_ctx, g2c[...], g2x[...])
    x2 = _ln(alpha * x1_ref[...] + g2 * f) * lg_ref[...] + lb_ref[...]
    if last:
        out_ref[...] = x2
    else:
        sh1 = jnp.where(is_ctx, sh1c[...], sh1x[...])
        sc1 = jnp.where(is_ctx, sc1c[...], sc1x[...])
        xo_ref[...] = x2
        h_ref[...] = (_ln(x2) * (1.0 + sc1) + sh1).astype(BF16)


def _combine(ys, dest, wts, x1, mod3, mod3_next, ln_g, ln_b, B, S, C, alpha, last):
    R, D = x1.shape
    Tb = C + S
    tc = 128
    nC = C // tc
    nT = Tb // tc
    j_off = nC if last else 0
    nJ = nT - j_off
    bi = lambda b, j, d: b
    ci = lambda b, j, d: B
    row = pl.BlockSpec((tc, D), lambda b, j, d: (b * nT + j + j_off, 0))
    vec = pl.BlockSpec((1, D), lambda b, j, d: (0, 0))
    if last:
        out_specs = pl.BlockSpec((tc, D), lambda b, j, d: (b * nJ + j, 0))
        out_shape = jax.ShapeDtypeStruct((B * S, D), F32)
    else:
        out_specs = [row, row]
        out_shape = [jax.ShapeDtypeStruct((R, D), F32), jax.ShapeDtypeStruct((R, D), BF16)]
    grid_spec = pltpu.PrefetchScalarGridSpec(
        num_scalar_prefetch=1,
        grid=(B, nJ),
        in_specs=[pl.BlockSpec(memory_space=pl.ANY),
                  pl.BlockSpec((tc, LANES), lambda b, j, d: (b * nT + j + j_off, 0)),
                  row,
                  _mod_spec(D, bi, 5), _mod_spec(D, ci, 5), _mod_spec(D, bi, 0), _mod_spec(D, ci, 0),
                  _mod_spec(D, bi, 1), _mod_spec(D, ci, 1), vec, vec],
        out_specs=out_specs,
        scratch_shapes=[pltpu.VMEM((2, TOP_K, tc, D), F32), pltpu.SemaphoreType.DMA((2,))],
    )
    return pl.pallas_call(
        functools.partial(_combine_kernel, tc=tc, nT=nT, j_off=j_off, nC=nC, alpha=alpha, last=last),
        grid_spec=grid_spec,
        out_shape=out_shape,
        compiler_params=_cp("arbitrary", "arbitrary"),
        name="combine",
    )(dest, ys, wts, x1, mod3, mod3, mod3_next, mod3_next, mod3_next, mod3_next,
      ln_g.reshape(1, D), ln_b.reshape(1, D))


def _rope_tables(S, C, head_dim):
    rows = S // GRID_W
    pos_row = jnp.repeat(jnp.arange(rows, dtype=jnp.int32), GRID_W).astype(F32)
    pos_col = jnp.tile(jnp.arange(GRID_W, dtype=jnp.int32), rows).astype(F32)
    n = head_dim // 4
    inv = ROPE_BASE ** (-jnp.arange(n, dtype=F32) / n)
    ang = jnp.concatenate([pos_row[:, None] * inv, pos_col[:, None] * inv], -1)
    cos, sin = jnp.cos(ang), jnp.sin(ang)
    reps = LANES // head_dim
    cos_f = jnp.tile(jnp.concatenate([cos, cos], -1), (1, reps))
    sin_f = jnp.tile(jnp.concatenate([-sin, sin], -1), (1, reps))
    cos_f = jnp.concatenate([jnp.ones((C, LANES), F32), cos_f], 0)
    sin_f = jnp.concatenate([jnp.zeros((C, LANES), F32), sin_f], 0)
    return cos_f, sin_f


def kernel(x, c, ctx, c_ctx, w_ada, b_ada, w_in, conv_w, conv_b, lru_wr, lru_br, lru_wi, lru_bi, lru_lambda, wa_sink, da_lq1, da_lk1, da_lq2, da_lk2, da_norm_g, w_br_a, w_br_b, w_br_c, w_out, b_out, ln1_g, ln1_b, router_w, router_b, exp_w_gate, exp_b_gate, exp_w_up, exp_b_up, exp_w_down, exp_b_down, ln2_g, ln2_b):
    B, S, D = x.shape
    C = ctx.shape[1]
    depth = w_ada.shape[0]
    Tb = C + S
    R = B * Tb
    d_in = w_in.shape[2]
    d_rnn = w_br_a.shape[1]
    wa_w = w_br_b.shape[1]
    da_w = w_br_c.shape[1]
    kv_w = (d_in - d_rnn - wa_w - 3 * da_w - 3 * D) // 2
    wa_heads = wa_w // WA_HEAD_DIM
    kvh_n = kv_w // WA_HEAD_DIM
    G = wa_heads // kvh_n
    da_heads = da_w // (2 * DA_HEAD_DIM)
    nrb = lru_wr.shape[2]
    E = router_w.shape[2]
    alpha = (2 * depth) ** 0.25
    assert B + 1 <= 8 and lru_wr.shape[3] == LANES and S % GRID_W == 0
    q_off = d_rnn
    k_off = q_off + wa_w
    v_off = k_off + kv_w
    qd_off = v_off + kv_w
    kd_off = qd_off + da_w
    vd_off = kd_off + da_w
    g_off = vd_off + da_w

    cos_wa, sin_wa = _rope_tables(S, C, WA_HEAD_DIM)
    cos_da, sin_da = _rope_tables(S, C, DA_HEAD_DIM)

    cc = jnp.concatenate([c, c_ctx[None, :], jnp.zeros((8 - B - 1, D), F32)], axis=0)
    b_ada3 = b_ada.reshape(depth, 1, 6 * D)
    mods = [_ada(cc, w_ada, b_ada3, l).reshape(8, 1, 6 * D) for l in range(depth)]

    xall, h = _ln0(x.reshape(B * S, D), ctx.reshape(B * C, D), mods[0], B, S, C)

    F = exp_w_gate.shape[3]
    wg3 = exp_w_gate.reshape(depth, E * D, F)
    wu3 = exp_w_up.reshape(depth, E * D, F)
    wd3 = exp_w_down.reshape(depth, E * F, D)
    bg_e = exp_b_gate.reshape(depth, E, 1, F)
    bu_e = exp_b_up.reshape(depth, E, 1, F)
    bd_e = exp_b_down.reshape(depth, E, 1, D)
    conv_b3 = conv_b.reshape(depth, 1, d_rnn)

    tm_moe = 256
    n_pairs = R * TOP_K
    n_tiles = n_pairs // tm_moe + E

    out = None
    for l in range(depth):
        last = l == depth - 1
        lam_init = 0.8 - 0.6 * math.exp(-0.3 * l)
        proj, (wg_b, wu_b, wd_b, wa_b, wb_b, wc_b) = _mm(
            h, w_in, l, 2176, 256,
            side=((wg3, E * D // 256), (wu3, E * D // 256), (wd3, E * F // 256),
                  (w_br_a, 16), (w_br_b, 16), (w_br_c, 16)), a_buffers=1)
        wg = jnp.concatenate([lru_wr[l, 0], lru_wi[l, 0], lru_wr[l, 1], lru_wi[l, 1]], axis=-1).astype(BF16)
        bg = jnp.stack([lru_br[l, 0].reshape(nrb, LANES), lru_bi[l, 0].reshape(nrb, LANES),
                        lru_br[l, 1].reshape(nrb, LANES), lru_bi[l, 1].reshape(nrb, LANES)], axis=1)
        bg = bg.reshape(nrb, 1, 4 * LANES)
        lam2 = jnp.stack([lru_lambda[l, 0].reshape(nrb, LANES), lru_lambda[l, 1].reshape(nrb, LANES)], axis=1)
        lam2 = lam2.reshape(nrb, 1, 2 * LANES)
        ya = _rglru(proj, conv_w, conv_b3, wg, bg, lam2, l, B, S, C, d_rnn)
        yb = _wattn(proj, wa_sink[l], cos_wa, sin_wa, B, S, C, q_off, k_off, v_off, kvh_n, G)
        lqk = jnp.stack([da_lq1[l], da_lk1[l], da_lq2[l], da_lk2[l]], axis=0)
        yc = _dattn(proj, cos_da, sin_da, da_norm_g[l].reshape(1, LANES), lqk, B, S, C,
                    qd_off, kd_off, vd_off, da_heads, lam_init)
        m = _merge(ya, yb, yc, wa_b, wb_b, wc_b, proj, g_off, D)
        mo, _ = _mm(m, w_out, l, 1088, 512)
        x1, h2, idx, wts = _post1(mo, xall, b_out[l], mods[l], ln1_g[l], ln1_b[l], router_w[l], router_b[l],
                                  B, S, C, alpha)
        te, tj0, tok, nact, dest = _dispatch(idx[:, :TOP_K], R, E, tm_moe, n_tiles)
        ys = _moe(h2, te, tj0, tok, nact, wg_b.reshape(E, D, F), bg_e, wu_b.reshape(E, D, F), bu_e,
                  wd_b.reshape(E, F, D), bd_e, l, tm_moe, n_tiles)
        nxt = mods[l + 1] if not last else mods[l]
        res = _combine(ys, dest, wts, x1, mods[l], nxt, ln2_g[l], ln2_b[l], B, S, C, alpha, last)
        if last:
            out = res.reshape(B, S, D)
        else:
            xall, h = res
    return out
```
